```python
import jax, jax.numpy as jnp
from jax import lax
import numpy as np

D_MODEL = 1024
BATCH = 32
SEQ = 2048
DEPTH = 1

CHUNK = 64
GMLP_BLOCK = 128
GMLP_WIDTH = D_MODEL
GMLP_GROUPS = 8
GMLP_GROUP_DIM = GMLP_WIDTH // GMLP_GROUPS
CONV_WIDTH = D_MODEL
CONV_K = 3
PEER_HEADS = 8
PEER_KEY_DIM = 256
PEER_HALF = PEER_KEY_DIM // 2
PEER_N_KEYS = 128
PEER_N_EXPERTS = PEER_N_KEYS * PEER_N_KEYS
PEER_TOPK = 16
PEER_TOKEN_BLOCK = 128
IN_COLS = 2 * GMLP_WIDTH + 3 * CONV_WIDTH + 2 * D_MODEL
DEEPNORM_ALPHA = (2.0 * DEPTH) ** 0.25
DEEPNORM_BETA = (8.0 * DEPTH) ** -0.25
LN_EPS = 1e-5

kernel_name = "hybrid_gmlp_shortconv_peer_deepnorm_adaln"


def _layer_norm(x, g=None, b=None):
    xf = x.astype(jnp.float32)
    mu = jnp.mean(xf, axis=-1, keepdims=True)
    var = jnp.mean(jnp.square(xf - mu), axis=-1, keepdims=True)
    y = (xf - mu) * lax.rsqrt(var + LN_EPS)
    if g is not None:
        y = y * g.astype(jnp.float32) + b.astype(jnp.float32)
    return y.astype(x.dtype)


def _adaln(x, shift, scale):
    return _layer_norm(x) * (1 + scale[:, None, :]) + shift[:, None, :]


def _gmlp_mixer(u, v, ln_g, ln_b, w_s, b_s):
    bsz, seq, _ = v.shape
    u = jax.nn.gelu(u)
    v = _layer_norm(jax.nn.gelu(v), ln_g, ln_b)
    pos = jnp.arange(GMLP_BLOCK)
    mask = (pos[None, :] // CHUNK) <= (pos[:, None] // CHUNK)
    w = jnp.where(mask[None], w_s, 0)
    vb = v.reshape(bsz, seq // GMLP_BLOCK, GMLP_BLOCK, GMLP_GROUPS, GMLP_GROUP_DIM)
    mixed = jnp.einsum('gij,bnjgc->bnigc', w, vb) + b_s.T[None, None, :, :, None]
    return u * mixed.reshape(bsz, seq, GMLP_WIDTH)


def _short_conv_mixer(g_b, g_c, z, conv_w):
    seq = z.shape[1]
    zc = jnp.pad(g_c * z, ((0, 0), (CONV_K - 1, 0), (0, 0)))
    y = sum(zc[:, k:k + seq] * conv_w[k] for k in range(CONV_K))
    return g_b * y


def _peer(h, w_q, sub_keys, expert_u, expert_v):
    bsz, seq, d = h.shape
    xt = h.reshape((bsz * seq) // PEER_TOKEN_BLOCK, PEER_TOKEN_BLOCK, d)

    def block(xb):
        t = xb.shape[0]
        q = (xb @ w_q).reshape(t, PEER_HEADS, 2, PEER_HALF)
        s = jnp.einsum('thpd,hpkd->thpk', q, sub_keys).astype(jnp.float32)
        s1, i1 = lax.top_k(s[:, :, 0], PEER_TOPK)
        s2, i2 = lax.top_k(s[:, :, 1], PEER_TOPK)
        cand_s = (s1[..., :, None] + s2[..., None, :]).reshape(t, PEER_HEADS, PEER_TOPK * PEER_TOPK)
        cand_i = (i1[..., :, None] * PEER_N_KEYS + i2[..., None, :]).reshape(t, PEER_HEADS, PEER_TOPK * PEER_TOPK)
        top_s, sel = lax.top_k(cand_s, PEER_TOPK)
        idx = jnp.take_along_axis(cand_i, sel, axis=-1)
        gate = jax.nn.softmax(top_s, axis=-1).astype(xb.dtype)
        u = jnp.take(expert_u, idx, axis=0)
        act = jax.nn.gelu(jnp.einsum('thkd,td->thk', u, xb))
        v = jnp.take(expert_v, idx, axis=0)
        return jnp.einsum('thk,thkd->td', gate * act, v)

    return lax.map(block, xt).reshape(bsz, seq, d)


def setup_inputs(seed: int = 0) -> dict:
    key = jax.random.key(seed)
    ks = jax.random.split(key, 24)
    nrm = lambda k, shape, s: jax.random.normal(k, shape, jnp.float32) * s
    L, D = DEPTH, D_MODEL
    gate_bias = jnp.concatenate([jnp.zeros((2 * D,)), jnp.ones((D,))] * 2)
    return {
        "x": nrm(ks[0], (BATCH, SEQ, D), 1.0),
        "c": nrm(ks[1], (BATCH, D), 1.0),
        "w_cond": nrm(ks[2], (L, D, 6 * D), 0.5 * D ** -0.5),
        "b_cond": gate_bias[None] + nrm(ks[3], (L, 6 * D), 0.02),
        "w_in": nrm(ks[4], (L, D, IN_COLS), D ** -0.5),
        "gmlp_ln_g": 1.0 + nrm(ks[5], (L, GMLP_WIDTH), 0.05),
        "gmlp_ln_b": nrm(ks[6], (L, GMLP_WIDTH), 0.02),
        "w_spatial": nrm(ks[7], (L, GMLP_GROUPS, GMLP_BLOCK, GMLP_BLOCK), GMLP_BLOCK ** -0.5),
        "b_spatial": 1.0 + nrm(ks[8], (L, GMLP_GROUPS, GMLP_BLOCK), 0.1),
        "conv_w": nrm(ks[9], (L, CONV_K, CONV_WIDTH), CONV_K ** -0.5),
        "p_a": nrm(ks[10], (L, GMLP_WIDTH, D), DEEPNORM_BETA * GMLP_WIDTH ** -0.5),
        "p_b": nrm(ks[11], (L, CONV_WIDTH, D), DEEPNORM_BETA * CONV_WIDTH ** -0.5),
        "w_o": nrm(ks[12], (L, D, D), DEEPNORM_BETA * D ** -0.5),
        "ln1_g": 1.0 + nrm(ks[13], (L, D), 0.05),
        "ln1_b": nrm(ks[14], (L, D), 0.02),
        "w_q_peer": nrm(ks[15], (L, D, PEER_HEADS * PEER_KEY_DIM), D ** -0.5),
        "sub_keys": nrm(ks[16], (L, PEER_HEADS, 2, PEER_N_KEYS, PEER_HALF), PEER_HALF ** -0.5),
        "expert_u": nrm(ks[17], (L, PEER_N_EXPERTS, D), D ** -0.5),
        "expert_v": nrm(ks[18], (L, PEER_N_EXPERTS, D), DEEPNORM_BETA),
        "ln2_g": 1.0 + nrm(ks[19], (L, D), 0.05),
        "ln2_b": nrm(ks[20], (L, D), 0.02),
    }


def reference(x, c, w_cond, b_cond, w_in, gmlp_ln_g, gmlp_ln_b, w_spatial, b_spatial,
              conv_w, p_a, p_b, w_o, ln1_g, ln1_b, w_q_peer, sub_keys, expert_u,
              expert_v, ln2_g, ln2_b):
    o_v = GMLP_WIDTH
    o_b = 2 * GMLP_WIDTH
    o_c = o_b + CONV_WIDTH
    o_z = o_c + CONV_WIDTH
    o_ga = o_z + CONV_WIDTH
    o_gb = o_ga + D_MODEL
    for l in range(DEPTH):
        mod = jax.nn.silu(c) @ w_cond[l] + b_cond[l]
        sh1, sc1, g1, sh2, sc2, g2 = jnp.split(mod, 6, axis=-1)

        h = _adaln(x, sh1, sc1)
        proj = h @ w_in[l]
        y_a = _gmlp_mixer(proj[..., :o_v], proj[..., o_v:o_b], gmlp_ln_g[l], gmlp_ln_b[l],
                          w_spatial[l], b_spatial[l])
        y_b = _short_conv_mixer(proj[..., o_b:o_c], proj[..., o_c:o_z], proj[..., o_z:o_ga],
                                conv_w[l])
        merged = (jax.nn.sigmoid(proj[..., o_ga:o_gb]) * (y_a @ p_a[l])
                  + jax.nn.sigmoid(proj[..., o_gb:]) * (y_b @ p_b[l]))
        mix = merged @ w_o[l]
        x = _layer_norm(DEEPNORM_ALPHA * x + g1[:, None, :] * mix, ln1_g[l], ln1_b[l])

        h2 = _adaln(x, sh2, sc2)
        ffn = _peer(h2, w_q_peer[l], sub_keys[l], expert_u[l], expert_v[l])
        x = _layer_norm(DEEPNORM_ALPHA * x + g2[:, None, :] * ffn, ln2_g[l], ln2_b[l])
    return x
```

```python
import functools

import jax
import jax.numpy as jnp
from jax import lax
from jax.experimental import pallas as pl
from jax.experimental.pallas import tpu as pltpu

D_MODEL = 1024
CHUNK = 64
GMLP_BLOCK = 128
GMLP_GROUPS = 8
CONV_K = 3
PEER_HEADS = 8
PEER_HALF = 128
PEER_N_KEYS = 128
PEER_TOPK = 16
N_PAIRS = PEER_HEADS * PEER_TOPK
DEEPNORM_ALPHA = 2.0 ** 0.25
LN_EPS = 1e-5

ROW_WORDS = 4
HALF_D = D_MODEL // 2
MIX_TS = 512
ROUTE_TT = 256
GATHER_TB = 64
GATHER_NT = 4
NORM_TS = 512
GATHER_VMEM_BYTES = 48 * 1024 * 1024
MIXER_VMEM_BYTES = 56 * 1024 * 1024

_F32 = jnp.float32
_BF16 = jnp.bfloat16
_NEG_INF = float("-inf")


def _ln(x):
    mu = jnp.mean(x, axis=-1, keepdims=True)
    xc = x - mu
    var = jnp.mean(xc * xc, axis=-1, keepdims=True)
    return xc * lax.rsqrt(var + LN_EPS)


def _gelu(x):
    return 0.5 * x * (1.0 + jnp.tanh(0.7978845608028654 * (x + 0.044715 * (x * x * x))))


def _sigmoid(x):
    return 1.0 / (1.0 + jnp.exp(-x))


def _resident():
    return pl.BlockSpec(memory_space=pltpu.VMEM)


def _cond_kernel(c_ref, w_ref, b_ref, o_ref):
    c = c_ref[...]
    a = (c * _sigmoid(c)).astype(_BF16)
    o_ref[...] = jnp.dot(a, w_ref[...].astype(_BF16), preferred_element_type=_F32) + b_ref[...]


def _cond_proj(c, w, b):
    bsz, d = c.shape
    n = w.shape[1]
    tn = 1024
    return pl.pallas_call(
        _cond_kernel,
        out_shape=jax.ShapeDtypeStruct((bsz, n), _F32),
        grid=(n // tn,),
        in_specs=[
            pl.BlockSpec((bsz, d), lambda j: (0, 0)),
            pl.BlockSpec((d, tn), lambda j: (0, j)),
            pl.BlockSpec((1, tn), lambda j: (0, j)),
        ],
        out_specs=pl.BlockSpec((bsz, tn), lambda j: (0, j)),
        compiler_params=pltpu.CompilerParams(dimension_semantics=("arbitrary",)),
        name="cond_proj",
    )(c, w, b.reshape(1, n))


def _mixer_kernel(x_ref, mod_ref, w_in_ref, gg_ref, gb_ref, ws_ref, bs_ref, cw_ref, pa_ref, pb_ref,
                  wo_ref, l1g_ref, l1b_ref, x1_ref, h2_ref, prev_ref, gu_ref, vn_ref, ya_ref):
    ts = x_ref.shape[1]
    d = D_MODEL

    @pl.when(pl.program_id(1) == 0)
    def _():
        prev_ref[...] = jnp.zeros_like(prev_ref)

    x = x_ref[0]
    mod = mod_ref[0]
    sh1, sc1, g1, sh2, sc2 = (mod[i:i + 1] for i in range(5))
    h = (_ln(x) * (1.0 + sc1) + sh1).astype(_BF16)

    def proj(j):
        return jnp.dot(h, w_in_ref[:, j * d:(j + 1) * d], preferred_element_type=_F32)

    gu_ref[...] = _gelu(proj(0))
    vn_ref[...] = (_ln(_gelu(proj(1))) * gg_ref[...] + gb_ref[...]).astype(_BF16)
    qi = lax.broadcasted_iota(jnp.int32, (GMLP_BLOCK, GMLP_BLOCK), 0) // CHUNK
    kj = lax.broadcasted_iota(jnp.int32, (GMLP_BLOCK, GMLP_BLOCK), 1) // CHUNK
    causal = kj <= qi
    for g in range(GMLP_GROUPS):
        wg = jnp.where(causal, ws_ref[g], jnp.zeros((), _BF16))
        cols = slice(g * GMLP_BLOCK, (g + 1) * GMLP_BLOCK)
        for n in range(ts // GMLP_BLOCK):
            rows = slice(n * GMLP_BLOCK, (n + 1) * GMLP_BLOCK)
            mixed = jnp.dot(wg, vn_ref[rows, cols], preferred_element_type=_F32) + bs_ref[g]
            ya_ref[rows, cols] = (gu_ref[rows, cols] * mixed).astype(_BF16)

    g_b = proj(2)
    zc = proj(3) * proj(4)
    ext = jnp.concatenate([prev_ref[...], zc], axis=0)
    prev_ref[...] = zc[ts - 8:, :]
    cw = cw_ref[...]
    y = ext[6:ts + 6, :] * cw[0:1] + ext[7:ts + 7, :] * cw[1:2] + zc * cw[2:3]
    y_b = (g_b * y).astype(_BF16)

    merged = (_sigmoid(proj(5)) * jnp.dot(ya_ref[...], pa_ref[...], preferred_element_type=_F32)
              + _sigmoid(proj(6)) * jnp.dot(y_b, pb_ref[...], preferred_element_type=_F32))
    mix = jnp.dot(merged.astype(_BF16), wo_ref[...], preferred_element_type=_F32)
    x1 = _ln(DEEPNORM_ALPHA * x + g1 * mix) * l1g_ref[...] + l1b_ref[...]
    x1_ref[0] = x1
    h2_ref[0] = _ln(x1) * (1.0 + sc2) + sh2


def _mixer(x, mod3, w_in, gg, gb, ws, bs, cw, pa, pb, wo, l1g, l1b):
    bsz, seq, d = x.shape
    ts = MIX_TS
    blk = pl.BlockSpec((1, ts, d), lambda b, s: (b, s, 0))
    return pl.pallas_call(
        _mixer_kernel,
        out_shape=(jax.ShapeDtypeStruct((bsz, seq, d), _F32),
                   jax.ShapeDtypeStruct((bsz, seq, d), _F32)),
        grid=(bsz, seq // ts),
        in_specs=[blk, pl.BlockSpec((1, 6, d), lambda b, s: (b, 0, 0))] + [_resident()] * 11,
        out_specs=(blk, blk),
        scratch_shapes=[
            pltpu.VMEM((8, d), _F32),
            pltpu.VMEM((ts, d), _F32),
            pltpu.VMEM((ts, d), _BF16),
            pltpu.VMEM((ts, d), _BF16),
        ],
        compiler_params=pltpu.CompilerParams(
            dimension_semantics=("arbitrary", "arbitrary"),
            vmem_limit_bytes=MIXER_VMEM_BYTES,
        ),
        name="mixer",
    )(x, mod3, w_in, gg, gb, ws, bs, cw, pa, pb, wo, l1g, l1b)


def _extract_top(vals, ids, payload, k):
    top_v, top_p = [], []
    big = jnp.int32(2 ** 30)
    for _ in range(k):
        m = jnp.max(vals, axis=0, keepdims=True)
        sel = jnp.min(jnp.where(vals == m, ids, big), axis=0, keepdims=True)
        hit = ids == sel
        top_v.append(m)
        if payload is None:
            top_p.append(sel)
        else:
            top_p.append(jnp.max(jnp.where(hit, payload, -1), axis=0, keepdims=True))
        vals = jnp.where(hit, _NEG_INF, vals)
    return top_v, top_p


def _route_kernel(h_ref, wq_ref, keys_ref, idx_ref, gate_ref, st_ref, it_ref):
    tt = h_ref.shape[0]
    q = jnp.dot(h_ref[...].astype(_BF16), wq_ref[...], preferred_element_type=_F32).astype(_BF16)
    key_id = lax.broadcasted_iota(jnp.int32, (PEER_N_KEYS, tt), 0)
    for hp in range(2 * PEER_HEADS):
        s = lax.dot_general(keys_ref[hp], q[:, hp * PEER_HALF:(hp + 1) * PEER_HALF],
                            (((1,), (1,)), ((), ())), preferred_element_type=_F32)
        top_v, top_i = _extract_top(s, key_id, None, PEER_TOPK)
        for r in range(PEER_TOPK):
            st_ref[hp, r:r + 1, :] = top_v[r]
            it_ref[hp, r:r + 1, :] = top_i[r]

    row = lax.broadcasted_iota(jnp.int32, (8, tt), 0)
    for hd in range(PEER_HEADS):
        s1, s2 = st_ref[2 * hd], st_ref[2 * hd + 1]
        i1, i2 = it_ref[2 * hd], it_ref[2 * hd + 1]
        vals, eids, cids = [], [], []
        for a in range(8):
            vals.append(s1[a:a + 1] + s2[0:8])
            eids.append(i1[a:a + 1] * PEER_N_KEYS + i2[0:8])
            cids.append(a * PEER_TOPK + row)
        vals.append(s1[0:1] + s2[8:16])
        eids.append(i1[0:1] * PEER_N_KEYS + i2[8:16])
        cids.append(8 + row)
        vals.append(s1[8:16] + s2[0:1])
        eids.append(i1[8:16] * PEER_N_KEYS + i2[0:1])
        cids.append((8 + row) * PEER_TOPK)
        top_s, top_e = _extract_top(jnp.concatenate(vals, axis=0), jnp.concatenate(cids, axis=0),
                                    jnp.concatenate(eids, axis=0), PEER_TOPK)
        ex = [jnp.exp(v - top_s[0]) for v in top_s]
        denom = functools.reduce(lambda p, r: p + r, ex)
        for r in range(PEER_TOPK):
            k = hd * PEER_TOPK + r
            idx_ref[k:k + 1, :] = top_e[r] * ROW_WORDS
            gate_ref[k:k + 1, :] = ex[r] / denom


def _route(h2, wq, keys):
    n_tok, d = h2.shape
    tt = ROUTE_TT
    return pl.pallas_call(
        _route_kernel,
        out_shape=(jax.ShapeDtypeStruct((N_PAIRS, n_tok), jnp.int32),
                   jax.ShapeDtypeStruct((N_PAIRS, n_tok), _F32)),
        grid=(n_tok // tt,),
        in_specs=[pl.BlockSpec((tt, d), lambda i: (i, 0)), _resident(), _resident()],
        out_specs=(pl.BlockSpec((N_PAIRS, tt), lambda i: (0, i)),
                   pl.BlockSpec((N_PAIRS, tt), lambda i: (0, i))),
        scratch_shapes=[
            pltpu.VMEM((2 * PEER_HEADS, PEER_TOPK, tt), _F32),
            pltpu.VMEM((2 * PEER_HEADS, PEER_TOPK, tt), jnp.int32),
        ],
        compiler_params=pltpu.CompilerParams(dimension_semantics=("arbitrary",)),
        name="peer_route",
    )(h2, wq, keys)


def _pack_table(w):
    n = w.shape[0]
    wb = w.astype(_BF16).reshape(n, 2, ROW_WORDS, 128)
    wb = jnp.transpose(wb, (0, 2, 3, 1))
    return lax.bitcast_convert_type(wb, jnp.int32).reshape(n * ROW_WORDS, 128)


def _gather_rows(idx_ref, t, tbl_ref, tile):
    row = idx_ref.at[t]
    for k in range(N_PAIRS):
        i = pl.multiple_of(row[k], ROW_WORDS)
        tile[pl.ds(ROW_WORDS * k, ROW_WORDS), :] = tbl_ref[pl.ds(i, ROW_WORDS), :]


def _chunk_major(tile):
    parts = [pltpu.bitcast(tile[pl.ds(r, N_PAIRS, stride=ROW_WORDS), :], _BF16)
             for r in range(ROW_WORDS)]
    return jnp.concatenate(parts, axis=1)


def _half_masks():
    lane = lax.broadcasted_iota(jnp.int32, (8, 2 * N_PAIRS), 1)
    sub = lax.broadcasted_iota(jnp.int32, (8, 2 * N_PAIRS), 0)
    return [(sub == 2 * n + (lane & 1)).astype(_F32) for n in range(GATHER_NT)]


def _pipelined_tokens(idx_ref, tbl_ref, tiles, consume):
    tb = idx_ref.shape[0]
    set_a, set_b = tiles[:GATHER_NT], tiles[GATHER_NT:]

    def stage(j, cur, nxt):
        t = GATHER_NT * j
        consume(j, t, cur, lambda n: _gather_rows(
            idx_ref, jnp.minimum(t + GATHER_NT + n, tb - 1), tbl_ref, nxt[n]))

    def body(j, carry):
        @pl.when(j % 2 == 0)
        def _():
            stage(j, set_a, set_b)

        @pl.when(j % 2 == 1)
        def _():
            stage(j, set_b, set_a)

        return carry

    for n in range(GATHER_NT):
        _gather_rows(idx_ref, n, tbl_ref, set_a[n])
    lax.fori_loop(0, tb // GATHER_NT, body, 0)


def _u_pass_kernel(idx_ref, x_ref, gate_ref, tbl_ref, out_ref, *scratch):
    tiles, zbufs = scratch[:-2], scratch[-2:]
    tb = idx_ref.shape[0]
    masks = _half_masks()

    def consume(j, t, cur, gather_next):
        xc = x_ref[j].astype(_BF16)
        acc = None
        for n in range(GATHER_NT):
            z = lax.dot_general(xc, _chunk_major(cur[n]), (((1,), (1,)), ((), ())),
                                preferred_element_type=_F32)
            acc = z * masks[n] if acc is None else acc + z * masks[n]
            gather_next(n)
        for c, zbuf in enumerate(zbufs):
            zbuf[pl.ds(pl.multiple_of(2 * t, 8), 8), :] = acc[:, c * 128:(c + 1) * 128]

    _pipelined_tokens(idx_ref, tbl_ref, tiles, consume)
    y = jnp.concatenate([zbuf[pl.ds(0, tb, stride=2), :] + zbuf[pl.ds(1, tb, stride=2), :]
                         for zbuf in zbufs], axis=1)
    lane = lax.broadcasted_iota(jnp.int32, y.shape, 1)
    other = jnp.where((lane & 1) == 0, pltpu.roll(y, 2 * N_PAIRS - 1, axis=1), pltpu.roll(y, 1, axis=1))
    out_ref[...] = gate_ref[...] * _gelu(y + other)


def _v_pass_kernel(idx_ref, w_ref, tbl_ref, out_ref, *tiles):
    masks = _half_masks()

    def consume(j, t, cur, gather_next):
        acc = None
        for n in range(GATHER_NT):
            wrow = jnp.broadcast_to(w_ref[pl.ds(t + n, 1), :], (8, 2 * N_PAIRS))
            lhs = (wrow * masks[n]).astype(_BF16)
            r = jnp.dot(lhs, _chunk_major(cur[n]), preferred_element_type=_F32)
            acc = r if acc is None else acc + r
            gather_next(n)
        out_ref[j] = acc

    _pipelined_tokens(idx_ref, tbl_ref, tiles, consume)


def _gather_call(kernel_fn, name, n_tok, in_specs, out_spec, out_shape, n_extra_scratch, args):
    tb = GATHER_TB
    tile = pltpu.VMEM((ROW_WORDS * N_PAIRS, 128), jnp.int32)
    return pl.pallas_call(
        kernel_fn,
        out_shape=out_shape,
        grid=(n_tok // tb,),
        in_specs=in_specs,
        out_specs=out_spec,
        scratch_shapes=[tile] * (2 * GATHER_NT) + n_extra_scratch,
        compiler_params=pltpu.CompilerParams(
            dimension_semantics=("arbitrary",),
            vmem_limit_bytes=GATHER_VMEM_BYTES,
        ),
        name=name,
    )(*args)


def _u_pass(idx, x4, gate2, tbl):
    n_tok = idx.shape[0]
    tb = GATHER_TB
    return _gather_call(
        _u_pass_kernel, "peer_u_pass", n_tok,
        [pl.BlockSpec((tb, N_PAIRS), lambda i: (i, 0), memory_space=pltpu.SMEM),
         pl.BlockSpec((tb // GATHER_NT, 8, HALF_D), lambda i: (i, 0, 0)),
         pl.BlockSpec((tb, 2 * N_PAIRS), lambda i: (i, 0)),
         _resident()],
        pl.BlockSpec((tb, 2 * N_PAIRS), lambda i: (i, 0)),
        jax.ShapeDtypeStruct((n_tok, 2 * N_PAIRS), _F32),
        [pltpu.VMEM((2 * tb, 128), _F32)] * 2,
        (idx, x4, gate2, tbl))


def _v_pass(idx, w2, tbl):
    n_tok = idx.shape[0]
    tb = GATHER_TB
    return _gather_call(
        _v_pass_kernel, "peer_v_pass", n_tok,
        [pl.BlockSpec((tb, N_PAIRS), lambda i: (i, 0), memory_space=pltpu.SMEM),
         pl.BlockSpec((tb, 2 * N_PAIRS), lambda i: (i, 0)),
         _resident()],
        pl.BlockSpec((tb // GATHER_NT, 8, HALF_D), lambda i: (i, 0, 0)),
        jax.ShapeDtypeStruct((n_tok // GATHER_NT, 8, HALF_D), _F32),
        [],
        (idx, w2, tbl))


def _final_kernel(x1_ref, ffn_ref, mod_ref, g_ref, b_ref, o_ref):
    g2 = mod_ref[0][5:6]
    o_ref[0] = _ln(DEEPNORM_ALPHA * x1_ref[0] + g2 * ffn_ref[0]) * g_ref[...] + b_ref[...]


def _final_norm(x1, ffn, mod3, g, b):
    bsz, seq, d = x1.shape
    ts = NORM_TS
    blk = pl.BlockSpec((1, ts, d), lambda i, s: (i, s, 0))
    vec = pl.BlockSpec((1, d), lambda i, s: (0, 0))
    return pl.pallas_call(
        _final_kernel,
        out_shape=jax.ShapeDtypeStruct((bsz, seq, d), _F32),
        grid=(bsz, seq // ts),
        in_specs=[blk, blk, pl.BlockSpec((1, 6, d), lambda i, s: (i, 0, 0)), vec, vec],
        out_specs=blk,
        compiler_params=pltpu.CompilerParams(dimension_semantics=("arbitrary", "arbitrary")),
        name="final_norm",
    )(x1, ffn, mod3, g, b)


def kernel(x, c, w_cond, b_cond, w_in, gmlp_ln_g, gmlp_ln_b, w_spatial, b_spatial, conv_w, p_a, p_b, w_o, ln1_g, ln1_b, w_q_peer, sub_keys, expert_u, expert_v, ln2_g, ln2_b):
    bsz, seq, d = x.shape
    n_tok = bsz * seq
    depth = w_cond.shape[0]
    row = lambda v: v.reshape(1, d)
    for l in range(depth):
        mod3 = _cond_proj(c, w_cond[l], b_cond[l]).reshape(bsz, 6, d)
        x1, h2 = _mixer(
            x, mod3, w_in[l].astype(_BF16), row(gmlp_ln_g[l]), row(gmlp_ln_b[l]),
            w_spatial[l].astype(_BF16), b_spatial[l][:, :, None], conv_w[l],
            p_a[l].astype(_BF16), p_b[l].astype(_BF16), w_o[l].astype(_BF16),
            row(ln1_g[l]), row(ln1_b[l]))
        h2 = h2.reshape(n_tok, d)
        keys = sub_keys[l].astype(_BF16).reshape(2 * PEER_HEADS, PEER_N_KEYS, PEER_HALF)
        idx_t, gate_t = _route(h2, w_q_peer[l].astype(_BF16), keys)
        idx = idx_t.T
        gate2 = jnp.repeat(gate_t.T, 2, axis=1)
        x4 = h2.reshape(n_tok // GATHER_NT, 8, HALF_D)
        w2 = _u_pass(idx, x4, gate2, _pack_table(expert_u[l]))
        ffn = _v_pass(idx, w2, _pack_table(expert_v[l])).reshape(bsz, seq, d)
        x = _final_norm(x1, ffn, mod3, row(ln2_g[l]), row(ln2_b[l]))
    return x
```

```python
import functools

import jax
import jax.numpy as jnp
from jax import lax
from jax.experimental import pallas as pl
from jax.experimental.pallas import tpu as pltpu

D_MODEL = 1024
CHUNK = 64
GMLP_BLOCK = 128
GMLP_GROUPS = 8
CONV_K = 3
PEER_HEADS = 8
PEER_HALF = 128
PEER_N_KEYS = 128
PEER_TOPK = 16
N_PAIRS = PEER_HEADS * PEER_TOPK
DEEPNORM_ALPHA = 2.0 ** 0.25
LN_EPS = 1e-5

ROW_WORDS = 4
HALF_D = D_MODEL // 2
MIX_TS = 512
ROUTE_TT = 256
GATHER_TB = 256
GATHER_NT = 4
NORM_TS = 512
MXU_LHS_ROWS = 16
MXU_ACC = (0, 4)
GATHER_VMEM_BYTES = 48 * 1024 * 1024
MIXER_VMEM_BYTES = 56 * 1024 * 1024

_F32 = jnp.float32
_BF16 = jnp.bfloat16
_NEG_INF = float("-inf")


def _ln(x):
    mu = jnp.mean(x, axis=-1, keepdims=True)
    xc = x - mu
    var = jnp.mean(xc * xc, axis=-1, keepdims=True)
    return xc * lax.rsqrt(var + LN_EPS)


def _gelu(x):
    return 0.5 * x * (1.0 + jnp.tanh(0.7978845608028654 * (x + 0.044715 * (x * x * x))))


def _sigmoid(x):
    return 1.0 / (1.0 + jnp.exp(-x))


def _resident():
    return pl.BlockSpec(memory_space=pltpu.VMEM)


def _cond_kernel(c_ref, w_ref, b_ref, o_ref):
    c = c_ref[...]
    a = (c * _sigmoid(c)).astype(_BF16)
    o_ref[...] = jnp.dot(a, w_ref[...].astype(_BF16), preferred_element_type=_F32) + b_ref[...]


def _cond_proj(c, w, b):
    bsz, d = c.shape
    n = w.shape[1]
    tn = 1024
    return pl.pallas_call(
        _cond_kernel,
        out_shape=jax.ShapeDtypeStruct((bsz, n), _F32),
        grid=(n // tn,),
        in_specs=[
            pl.BlockSpec((bsz, d), lambda j: (0, 0)),
            pl.BlockSpec((d, tn), lambda j: (0, j)),
            pl.BlockSpec((1, tn), lambda j: (0, j)),
        ],
        out_specs=pl.BlockSpec((bsz, tn), lambda j: (0, j)),
        compiler_params=pltpu.CompilerParams(dimension_semantics=("arbitrary",)),
        name="cond_proj",
    )(c, w, b.reshape(1, n))


def _mixer_kernel(x_ref, mod_ref, w_in_ref, gg_ref, gb_ref, ws_ref, bs_ref, cw_ref, pa_ref, pb_ref,
                  wo_ref, l1g_ref, l1b_ref, x1_ref, h2_ref, prev_ref, gu_ref, vn_ref, ya_ref):
    ts = x_ref.shape[1]
    d = D_MODEL

    @pl.when(pl.program_id(1) == 0)
    def _():
        prev_ref[...] = jnp.zeros_like(prev_ref)

    x = x_ref[0]
    mod = mod_ref[0]
    sh1, sc1, g1, sh2, sc2 = (mod[i:i + 1] for i in range(5))
    h = (_ln(x) * (1.0 + sc1) + sh1).astype(_BF16)

    def proj(j):
        return jnp.dot(h, w_in_ref[:, j * d:(j + 1) * d], preferred_element_type=_F32)

    gu_ref[...] = _gelu(proj(0))
    vn_ref[...] = (_ln(_gelu(proj(1))) * gg_ref[...] + gb_ref[...]).astype(_BF16)
    qi = lax.broadcasted_iota(jnp.int32, (GMLP_BLOCK, GMLP_BLOCK), 0) // CHUNK
    kj = lax.broadcasted_iota(jnp.int32, (GMLP_BLOCK, GMLP_BLOCK), 1) // CHUNK
    causal = kj <= qi
    for g in range(GMLP_GROUPS):
        wg = jnp.where(causal, ws_ref[g], jnp.zeros((), _BF16))
        cols = slice(g * GMLP_BLOCK, (g + 1) * GMLP_BLOCK)
        for n in range(ts // GMLP_BLOCK):
            rows = slice(n * GMLP_BLOCK, (n + 1) * GMLP_BLOCK)
            mixed = jnp.dot(wg, vn_ref[rows, cols], preferred_element_type=_F32) + bs_ref[g]
            ya_ref[rows, cols] = (gu_ref[rows, cols] * mixed).astype(_BF16)

    g_b = proj(2)
    zc = proj(3) * proj(4)
    ext = jnp.concatenate([prev_ref[...], zc], axis=0)
    prev_ref[...] = zc[ts - 8:, :]
    cw = cw_ref[...]
    y = ext[6:ts + 6, :] * cw[0:1] + ext[7:ts + 7, :] * cw[1:2] + zc * cw[2:3]
    y_b = (g_b * y).astype(_BF16)

    merged = (_sigmoid(proj(5)) * jnp.dot(ya_ref[...], pa_ref[...], preferred_element_type=_F32)
              + _sigmoid(proj(6)) * jnp.dot(y_b, pb_ref[...], preferred_element_type=_F32))
    mix = jnp.dot(merged.astype(_BF16), wo_ref[...], preferred_element_type=_F32)
    x1 = _ln(DEEPNORM_ALPHA * x + g1 * mix) * l1g_ref[...] + l1b_ref[...]
    x1_ref[0] = x1
    h2_ref[0] = _ln(x1) * (1.0 + sc2) + sh2


def _mixer(x, mod3, w_in, gg, gb, ws, bs, cw, pa, pb, wo, l1g, l1b):
    bsz, seq, d = x.shape
    ts = MIX_TS
    blk = pl.BlockSpec((1, ts, d), lambda b, s: (b, s, 0))
    return pl.pallas_call(
        _mixer_kernel,
        out_shape=(jax.ShapeDtypeStruct((bsz, seq, d), _F32),
                   jax.ShapeDtypeStruct((bsz, seq, d), _F32)),
        grid=(bsz, seq // ts),
        in_specs=[blk, pl.BlockSpec((1, 6, d), lambda b, s: (b, 0, 0))] + [_resident()] * 11,
        out_specs=(blk, blk),
        scratch_shapes=[
            pltpu.VMEM((8, d), _F32),
            pltpu.VMEM((ts, d), _F32),
            pltpu.VMEM((ts, d), _BF16),
            pltpu.VMEM((ts, d), _BF16),
        ],
        compiler_params=pltpu.CompilerParams(
            dimension_semantics=("arbitrary", "arbitrary"),
            vmem_limit_bytes=MIXER_VMEM_BYTES,
        ),
        name="mixer",
    )(x, mod3, w_in, gg, gb, ws, bs, cw, pa, pb, wo, l1g, l1b)


def _extract_top(vals, ids, payload, k):
    top_v, top_p = [], []
    big = jnp.int32(2 ** 30)
    for _ in range(k):
        m = jnp.max(vals, axis=0, keepdims=True)
        sel = jnp.min(jnp.where(vals == m, ids, big), axis=0, keepdims=True)
        hit = ids == sel
        top_v.append(m)
        if payload is None:
            top_p.append(sel)
        else:
            top_p.append(jnp.max(jnp.where(hit, payload, -1), axis=0, keepdims=True))
        vals = jnp.where(hit, _NEG_INF, vals)
    return top_v, top_p


def _route_kernel(h_ref, wq_ref, keys_ref, idx_ref, gate_ref, st_ref, it_ref):
    tt = h_ref.shape[0]
    q = jnp.dot(h_ref[...].astype(_BF16), wq_ref[...], preferred_element_type=_F32).astype(_BF16)
    key_id = lax.broadcasted_iota(jnp.int32, (PEER_N_KEYS, tt), 0)
    for hp in range(2 * PEER_HEADS):
        s = lax.dot_general(keys_ref[hp], q[:, hp * PEER_HALF:(hp + 1) * PEER_HALF],
                            (((1,), (1,)), ((), ())), preferred_element_type=_F32)
        top_v, top_i = _extract_top(s, key_id, None, PEER_TOPK)
        for r in range(PEER_TOPK):
            st_ref[hp, r:r + 1, :] = top_v[r]
            it_ref[hp, r:r + 1, :] = top_i[r]

    row = lax.broadcasted_iota(jnp.int32, (8, tt), 0)
    for hd in range(PEER_HEADS):
        s1, s2 = st_ref[2 * hd], st_ref[2 * hd + 1]
        i1, i2 = it_ref[2 * hd], it_ref[2 * hd + 1]
        vals, eids, cids = [], [], []
        for a in range(8):
            vals.append(s1[a:a + 1] + s2[0:8])
            eids.append(i1[a:a + 1] * PEER_N_KEYS + i2[0:8])
            cids.append(a * PEER_TOPK + row)
        vals.append(s1[0:1] + s2[8:16])
        eids.append(i1[0:1] * PEER_N_KEYS + i2[8:16])
        cids.append(8 + row)
        vals.append(s1[8:16] + s2[0:1])
        eids.append(i1[8:16] * PEER_N_KEYS + i2[0:1])
        cids.append((8 + row) * PEER_TOPK)
        top_s, top_e = _extract_top(jnp.concatenate(vals, axis=0), jnp.concatenate(cids, axis=0),
                                    jnp.concatenate(eids, axis=0), PEER_TOPK)
        ex = [jnp.exp(v - top_s[0]) for v in top_s]
        denom = functools.reduce(lambda p, r: p + r, ex)
        for r in range(PEER_TOPK):
            k = hd * PEER_TOPK + r
            idx_ref[k:k + 1, :] = top_e[r] * ROW_WORDS
            gate_ref[k:k + 1, :] = ex[r] / denom


def _route(h2, wq, keys):
    n_tok, d = h2.shape
    tt = ROUTE_TT
    return pl.pallas_call(
        _route_kernel,
        out_shape=(jax.ShapeDtypeStruct((N_PAIRS, n_tok), jnp.int32),
                   jax.ShapeDtypeStruct((N_PAIRS, n_tok), _F32)),
        grid=(n_tok // tt,),
        in_specs=[pl.BlockSpec((tt, d), lambda i: (i, 0)), _resident(), _resident()],
        out_specs=(pl.BlockSpec((N_PAIRS, tt), lambda i: (0, i)),
                   pl.BlockSpec((N_PAIRS, tt), lambda i: (0, i))),
        scratch_shapes=[
            pltpu.VMEM((2 * PEER_HEADS, PEER_TOPK, tt), _F32),
            pltpu.VMEM((2 * PEER_HEADS, PEER_TOPK, tt), jnp.int32),
        ],
        compiler_params=pltpu.CompilerParams(dimension_semantics=("arbitrary",)),
        name="peer_route",
    )(h2, wq, keys)


def _pack_table(w):
    n = w.shape[0]
    wb = w.astype(_BF16).reshape(n, 2, ROW_WORDS, 128)
    wb = jnp.transpose(wb, (0, 2, 3, 1))
    return lax.bitcast_convert_type(wb, jnp.int32).reshape(n * ROW_WORDS, 128)


def _gather_rows(idx_ref, tokens, tbl_ref, slab_tiles, lo=0, hi=N_PAIRS):
    rows = [idx_ref.at[t] for t in tokens]
    for k in range(lo, hi):
        for row, tile in zip(rows, slab_tiles):
            i = pl.multiple_of(row[k], ROW_WORDS)
            tile[pl.ds(ROW_WORDS * k, ROW_WORDS), :] = tbl_ref[pl.ds(i, ROW_WORDS), :]


def _rhs_half(tile, h):
    parts = [pltpu.bitcast(tile[pl.ds(2 * h + r, N_PAIRS, stride=ROW_WORDS), :], _BF16)
             for r in range(2)]
    return jnp.concatenate(parts, axis=1)


def _mxu_accumulate(acc, lhs_halves, tile, reg, transpose):
    for h in range(2):
        pltpu.matmul_push_rhs(_rhs_half(tile, h), staging_register=reg, mxu_index=h, transpose=transpose)
        pltpu.matmul_acc_lhs(acc, lhs_halves[h], mxu_index=h, load_staged_rhs=reg)


def _mxu_pop(acc):
    return [pltpu.matmul_pop(acc, (MXU_LHS_ROWS, 256), _F32, mxu_index=h) for h in range(2)]


def _slab_lhs(rows8):
    return jnp.concatenate([rows8, jnp.zeros_like(rows8)], axis=0).astype(_BF16)


def _row_masks():
    sub = lax.broadcasted_iota(jnp.int32, (8, 2 * N_PAIRS), 0)
    return [((sub >> 1) == n).astype(_F32) for n in range(GATHER_NT)]


def _parity_mask():
    lane = lax.broadcasted_iota(jnp.int32, (8, 2 * N_PAIRS), 1)
    sub = lax.broadcasted_iota(jnp.int32, (8, 2 * N_PAIRS), 0)
    return ((sub & 1) == (lane & 1)).astype(_F32)


def _pipelined_tokens(idx_ref, tbl_ref, tiles, accumulate, drain):
    tb = idx_ref.shape[0]
    n_stage = tb // GATHER_NT
    assert n_stage % 2 == 0
    set_a, set_b = tiles[:GATHER_NT], tiles[GATHER_NT:]
    half = N_PAIRS // 2

    def half_stage(trip, first, cur, nxt, acc, acc_prev):
        j = lax.shift_right_logical(trip, 1)
        t = GATHER_NT * j
        ahead = [jnp.minimum(t + GATHER_NT + n, tb - 1) for n in range(GATHER_NT)]
        if not first:
            drain(jnp.maximum(j - 1, 0), acc_prev)
        for n in ((0, 1) if first else (2, 3)):
            accumulate(j, t + n, n, cur[n], acc, n % 2)
        lo = 0 if first else half
        _gather_rows(idx_ref, ahead, tbl_ref, nxt, lo, lo + half)

    def body(trip, carry):
        variants = ((True, set_a, set_b, 0), (False, set_a, set_b, 0),
                    (True, set_b, set_a, 1), (False, set_b, set_a, 1))
        for q, (first, cur, nxt, p) in enumerate(variants):
            @pl.when((trip & 3) == q)
            def _():
                half_stage(trip, first, cur, nxt, MXU_ACC[p], MXU_ACC[1 - p])

        return carry

    @pl.when(pl.program_id(0) == 0)
    def _():
        for acc in MXU_ACC:
            _mxu_pop(acc)

    _gather_rows(idx_ref, list(range(GATHER_NT)), tbl_ref, set_a)
    lax.fori_loop(0, 2 * n_stage, body, 0)
    drain(n_stage - 1, MXU_ACC[(n_stage - 1) % 2])


def _u_pass_kernel(idx_ref, x_ref, gate_ref, tbl_ref, out_ref, *scratch):
    tiles, zbufs = scratch[:-2], scratch[-2:]
    tb = idx_ref.shape[0]
    row_masks = _row_masks()
    parity = _parity_mask()

    def accumulate(slab, t, n, tile, acc, reg):
        lhs = _slab_lhs(x_ref[slab] * row_masks[n][:, :1])
        _mxu_accumulate(acc, [lhs[:, :256], lhs[:, 256:]], tile, reg, transpose=True)

    def drain(slab, acc):
        z0, z1 = _mxu_pop(acc)
        z = (z0[:8] + z1[:8]) * parity
        rows = pl.ds(pl.multiple_of(8 * slab, 8), 8)
        for c, zbuf in enumerate(zbufs):
            zbuf[rows, :] = z[:, c * 128:(c + 1) * 128]

    _pipelined_tokens(idx_ref, tbl_ref, tiles, accumulate, drain)
    y = jnp.concatenate([zbuf[pl.ds(0, tb, stride=2), :] + zbuf[pl.ds(1, tb, stride=2), :]
                         for zbuf in zbufs], axis=1)
    lane = lax.broadcasted_iota(jnp.int32, y.shape, 1)
    other = jnp.where((lane & 1) == 0, pltpu.roll(y, 2 * N_PAIRS - 1, axis=1), pltpu.roll(y, 1, axis=1))
    out_ref[...] = gate_ref[...] * _gelu(y + other)


def _v_pass_kernel(idx_ref, w_ref, tbl_ref, out_ref, *tiles):
    row_masks = _row_masks()
    parity = _parity_mask()

    def accumulate(slab, t, n, tile, acc, reg):
        wrow = jnp.broadcast_to(w_ref[pl.ds(t, 1), :], (8, 2 * N_PAIRS))
        lhs = _slab_lhs(wrow * (row_masks[n] * parity))
        _mxu_accumulate(acc, [lhs, lhs], tile, reg, transpose=False)

    def drain(slab, acc):
        r0, r1 = _mxu_pop(acc)
        out_ref[slab] = jnp.concatenate([r0[:8], r1[:8]], axis=1)

    _pipelined_tokens(idx_ref, tbl_ref, tiles, accumulate, drain)


def _gather_call(kernel_fn, name, n_tok, in_specs, out_spec, out_shape, n_extra_scratch, args):
    tb = GATHER_TB
    tile = pltpu.VMEM((ROW_WORDS * N_PAIRS, 128), jnp.int32)
    return pl.pallas_call(
        kernel_fn,
        out_shape=out_shape,
        grid=(n_tok // tb,),
        in_specs=in_specs,
        out_specs=out_spec,
        scratch_shapes=[tile] * (2 * GATHER_NT) + n_extra_scratch,
        compiler_params=pltpu.CompilerParams(
            dimension_semantics=("arbitrary",),
            vmem_limit_bytes=GATHER_VMEM_BYTES,
        ),
        name=name,
    )(*args)


def _u_pass(idx, x4, gate2, tbl):
    n_tok = idx.shape[0]
    tb = GATHER_TB
    return _gather_call(
        _u_pass_kernel, "peer_u_pass", n_tok,
        [pl.BlockSpec((tb, N_PAIRS), lambda i: (i, 0), memory_space=pltpu.SMEM),
         pl.BlockSpec((tb // GATHER_NT, 8, HALF_D), lambda i: (i, 0, 0)),
         pl.BlockSpec((tb, 2 * N_PAIRS), lambda i: (i, 0)),
         _resident()],
        pl.BlockSpec((tb, 2 * N_PAIRS), lambda i: (i, 0)),
        jax.ShapeDtypeStruct((n_tok, 2 * N_PAIRS), _F32),
        [pltpu.VMEM((2 * tb, 128), _F32)] * 2,
        (idx, x4, gate2, tbl))


def _v_pass(idx, w2, tbl):
    n_tok = idx.shape[0]
    tb = GATHER_TB
    return _gather_call(
        _v_pass_kernel, "peer_v_pass", n_tok,
        [pl.BlockSpec((tb, N_PAIRS), lambda i: (i, 0), memory_space=pltpu.SMEM),
         pl.BlockSpec((tb, 2 * N_PAIRS), lambda i: (i, 0)),
         _resident()],
        pl.BlockSpec((tb // GATHER_NT, 8, HALF_D), lambda i: (i, 0, 0)),
        jax.ShapeDtypeStruct((n_tok // GATHER_NT, 8, HALF_D), _F32),
        [],
        (idx, w2, tbl))


def _final_kernel(x1_ref, ffn_ref, mod_ref, g_ref, b_ref, o_ref):
    g2 = mod_ref[0][5:6]
    o_ref[0] = _ln(DEEPNORM_ALPHA * x1_ref[0] + g2 * ffn_ref[0]) * g_ref[...] + b_ref[...]


def _final_norm(x1, ffn, mod3, g, b):
    bsz, seq, d = x1.shape
    ts = NORM_TS
    blk = pl.BlockSpec((1, ts, d), lambda i, s: (i, s, 0))
    vec = pl.BlockSpec((1, d), lambda i, s: (0, 0))
    return pl.pallas_call(
        _final_kernel,
        out_shape=jax.ShapeDtypeStruct((bsz, seq, d), _F32),
        grid=(bsz, seq // ts),
        in_specs=[blk, blk, pl.BlockSpec((1, 6, d), lambda i, s: (i, 0, 0)), vec, vec],
        out_specs=blk,
        compiler_params=pltpu.CompilerParams(dimension_semantics=("arbitrary", "arbitrary")),
        name="final_norm",
    )(x1, ffn, mod3, g, b)


def kernel(x, c, w_cond, b_cond, w_in, gmlp_ln_g, gmlp_ln_b, w_spatial, b_spatial, conv_w, p_a, p_b, w_o, ln1_g, ln1_b, w_q_peer, sub_keys, expert_u, expert_v, ln2_g, ln2_b):
    bsz, seq, d = x.shape
    n_tok = bsz * seq
    depth = w_cond.shape[0]
    row = lambda v: v.reshape(1, d)
    for l in range(depth):
        mod3 = _cond_proj(c, w_cond[l], b_cond[l]).reshape(bsz, 6, d)
        x1, h2 = _mixer(
            x, mod3, w_in[l].astype(_BF16), row(gmlp_ln_g[l]), row(gmlp_ln_b[l]),
            w_spatial[l].astype(_BF16), b_spatial[l][:, :, None], conv_w[l],
            p_a[l].astype(_BF16), p_b[l].astype(_BF16), w_o[l].astype(_BF16),
            row(ln1_g[l]), row(ln1_b[l]))
        h2 = h2.reshape(n_tok, d)
        keys = sub_keys[l].astype(_BF16).reshape(2 * PEER_HEADS, PEER_N_KEYS, PEER_HALF)
        idx_t, gate_t = _route(h2, w_q_peer[l].astype(_BF16), keys)
        idx = idx_t.T
        gate2 = jnp.repeat(gate_t.T, 2, axis=1)
        x4 = h2.reshape(n_tok // GATHER_NT, 8, HALF_D)
        w2 = _u_pass(idx, x4, gate2, _pack_table(expert_u[l]))
        ffn = _v_pass(idx, w2, _pack_table(expert_v[l])).reshape(bsz, seq, d)
        x = _final_norm(x1, ffn, mod3, row(ln2_g[l]), row(ln2_b[l]))
    return x
```

```python
import functools

import jax
import jax.numpy as jnp
from jax import lax
from jax.experimental import pallas as pl
from jax.experimental.pallas import tpu as pltpu

D_MODEL = 1024
CHUNK = 64
GMLP_BLOCK = 128
GMLP_GROUPS = 8
CONV_K = 3
PEER_HEADS = 8
PEER_HALF = 128
PEER_N_KEYS = 128
PEER_TOPK = 16
N_PAIRS = PEER_HEADS * PEER_TOPK
DEEPNORM_ALPHA = 2.0 ** 0.25
LN_EPS = 1e-5

ROW_WORDS = 4
HALF_D = D_MODEL // 2
MIX_TS = 512
ROUTE_TT = 256
GATHER_TB = 256
GATHER_NT = 4
NORM_TS = 512
MXU_LHS_ROWS = 16
MXU_ACC = (0, 4)
GATHER_VMEM_BYTES = 48 * 1024 * 1024
MIXER_VMEM_BYTES = 56 * 1024 * 1024

_F32 = jnp.float32
_BF16 = jnp.bfloat16
_NEG_INF = float("-inf")


def _ln(x):
    mu = jnp.mean(x, axis=-1, keepdims=True)
    xc = x - mu
    var = jnp.mean(xc * xc, axis=-1, keepdims=True)
    return xc * lax.rsqrt(var + LN_EPS)


def _gelu(x):
    return 0.5 * x * (1.0 + jnp.tanh(0.7978845608028654 * (x + 0.044715 * (x * x * x))))


def _sigmoid(x):
    return 1.0 / (1.0 + jnp.exp(-x))


def _resident():
    return pl.BlockSpec(memory_space=pltpu.VMEM)


def _cond_kernel(c_ref, w_ref, b_ref, o_ref):
    c = c_ref[...]
    a = (c * _sigmoid(c)).astype(_BF16)
    o_ref[...] = jnp.dot(a, w_ref[...].astype(_BF16), preferred_element_type=_F32) + b_ref[...]


def _cond_proj(c, w, b):
    bsz, d = c.shape
    n = w.shape[1]
    tn = 1024
    return pl.pallas_call(
        _cond_kernel,
        out_shape=jax.ShapeDtypeStruct((bsz, n), _F32),
        grid=(n // tn,),
        in_specs=[
            pl.BlockSpec((bsz, d), lambda j: (0, 0)),
            pl.BlockSpec((d, tn), lambda j: (0, j)),
            pl.BlockSpec((1, tn), lambda j: (0, j)),
        ],
        out_specs=pl.BlockSpec((bsz, tn), lambda j: (0, j)),
        compiler_params=pltpu.CompilerParams(dimension_semantics=("arbitrary",)),
        name="cond_proj",
    )(c, w, b.reshape(1, n))


def _mixer_kernel(x_ref, mod_ref, w_in_ref, gg_ref, gb_ref, ws_ref, bs_ref, cw_ref, pa_ref, pb_ref,
                  wo_ref, l1g_ref, l1b_ref, x1_ref, h2_ref, prev_ref, gu_ref, vn_ref, ya_ref):
    ts = x_ref.shape[1]
    d = D_MODEL

    @pl.when(pl.program_id(1) == 0)
    def _():
        prev_ref[...] = jnp.zeros_like(prev_ref)

    x = x_ref[0]
    mod = mod_ref[0]
    sh1, sc1, g1, sh2, sc2 = (mod[i:i + 1] for i in range(5))
    h = (_ln(x) * (1.0 + sc1) + sh1).astype(_BF16)

    def proj(j):
        return jnp.dot(h, w_in_ref[:, j * d:(j + 1) * d], preferred_element_type=_F32)

    gu_ref[...] = _gelu(proj(0))
    vn_ref[...] = (_ln(_gelu(proj(1))) * gg_ref[...] + gb_ref[...]).astype(_BF16)
    qi = lax.broadcasted_iota(jnp.int32, (GMLP_BLOCK, GMLP_BLOCK), 0) // CHUNK
    kj = lax.broadcasted_iota(jnp.int32, (GMLP_BLOCK, GMLP_BLOCK), 1) // CHUNK
    causal = kj <= qi
    for g in range(GMLP_GROUPS):
        wg = jnp.where(causal, ws_ref[g], jnp.zeros((), _BF16))
        cols = slice(g * GMLP_BLOCK, (g + 1) * GMLP_BLOCK)
        for n in range(ts // GMLP_BLOCK):
            rows = slice(n * GMLP_BLOCK, (n + 1) * GMLP_BLOCK)
            mixed = jnp.dot(wg, vn_ref[rows, cols], preferred_element_type=_F32) + bs_ref[g]
            ya_ref[rows, cols] = (gu_ref[rows, cols] * mixed).astype(_BF16)

    g_b = proj(2)
    zc = proj(3) * proj(4)
    ext = jnp.concatenate([prev_ref[...], zc], axis=0)
    prev_ref[...] = zc[ts - 8:, :]
    cw = cw_ref[...]
    y = ext[6:ts + 6, :] * cw[0:1] + ext[7:ts + 7, :] * cw[1:2] + zc * cw[2:3]
    y_b = (g_b * y).astype(_BF16)

    merged = (_sigmoid(proj(5)) * jnp.dot(ya_ref[...], pa_ref[...], preferred_element_type=_F32)
              + _sigmoid(proj(6)) * jnp.dot(y_b, pb_ref[...], preferred_element_type=_F32))
    mix = jnp.dot(merged.astype(_BF16), wo_ref[...], preferred_element_type=_F32)
    x1 = _ln(DEEPNORM_ALPHA * x + g1 * mix) * l1g_ref[...] + l1b_ref[...]
    x1_ref[0] = x1
    h2_ref[0] = _ln(x1) * (1.0 + sc2) + sh2


def _mixer(x, mod3, w_in, gg, gb, ws, bs, cw, pa, pb, wo, l1g, l1b):
    bsz, seq, d = x.shape
    ts = MIX_TS
    blk = pl.BlockSpec((1, ts, d), lambda b, s: (b, s, 0))
    return pl.pallas_call(
        _mixer_kernel,
        out_shape=(jax.ShapeDtypeStruct((bsz, seq, d), _F32),
                   jax.ShapeDtypeStruct((bsz, seq, d), _F32)),
        grid=(bsz, seq // ts),
        in_specs=[blk, pl.BlockSpec((1, 6, d), lambda b, s: (b, 0, 0))] + [_resident()] * 11,
        out_specs=(blk, blk),
        scratch_shapes=[
            pltpu.VMEM((8, d), _F32),
            pltpu.VMEM((ts, d), _F32),
            pltpu.VMEM((ts, d), _BF16),
            pltpu.VMEM((ts, d), _BF16),
        ],
        compiler_params=pltpu.CompilerParams(
            dimension_semantics=("arbitrary", "arbitrary"),
            vmem_limit_bytes=MIXER_VMEM_BYTES,
        ),
        name="mixer",
    )(x, mod3, w_in, gg, gb, ws, bs, cw, pa, pb, wo, l1g, l1b)


def _extract_top(vals, ids, payload, k):
    top_v, top_p = [], []
    big = jnp.int32(2 ** 30)
    for _ in range(k):
        m = jnp.max(vals, axis=0, keepdims=True)
        sel = jnp.min(jnp.where(vals == m, ids, big), axis=0, keepdims=True)
        hit = ids == sel
        top_v.append(m)
        if payload is None:
            top_p.append(sel)
        else:
            top_p.append(jnp.max(jnp.where(hit, payload, -1), axis=0, keepdims=True))
        vals = jnp.where(hit, _NEG_INF, vals)
    return top_v, top_p


def _route_kernel(h_ref, wq_ref, keys_ref, idx_ref, gate_ref, st_ref, it_ref):
    tt = h_ref.shape[0]
    q = jnp.dot(h_ref[...].astype(_BF16), wq_ref[...], preferred_element_type=_F32).astype(_BF16)
    key_id = lax.broadcasted_iota(jnp.int32, (PEER_N_KEYS, tt), 0)
    for hp in range(2 * PEER_HEADS):
        s = lax.dot_general(keys_ref[hp], q[:, hp * PEER_HALF:(hp + 1) * PEER_HALF],
                            (((1,), (1,)), ((), ())), preferred_element_type=_F32)
        top_v, top_i = _extract_top(s, key_id, None, PEER_TOPK)
        for r in range(PEER_TOPK):
            st_ref[hp, r:r + 1, :] = top_v[r]
            it_ref[hp, r:r + 1, :] = top_i[r]

    row = lax.broadcasted_iota(jnp.int32, (8, tt), 0)
    for hd in range(PEER_HEADS):
        s1, s2 = st_ref[2 * hd], st_ref[2 * hd + 1]
        i1, i2 = it_ref[2 * hd], it_ref[2 * hd + 1]
        vals, eids, cids = [], [], []
        for a in range(8):
            vals.append(s1[a:a + 1] + s2[0:8])
            eids.append(i1[a:a + 1] * PEER_N_KEYS + i2[0:8])
            cids.append(a * PEER_TOPK + row)
        vals.append(s1[0:1] + s2[8:16])
        eids.append(i1[0:1] * PEER_N_KEYS + i2[8:16])
        cids.append(8 + row)
        vals.append(s1[8:16] + s2[0:1])
        eids.append(i1[8:16] * PEER_N_KEYS + i2[0:1])
        cids.append((8 + row) * PEER_TOPK)
        top_s, top_e = _extract_top(jnp.concatenate(vals, axis=0), jnp.concatenate(cids, axis=0),
                                    jnp.concatenate(eids, axis=0), PEER_TOPK)
        ex = [jnp.exp(v - top_s[0]) for v in top_s]
        denom = functools.reduce(lambda p, r: p + r, ex)
        for r in range(PEER_TOPK):
            k = hd * PEER_TOPK + r
            idx_ref[k:k + 1, :] = top_e[r] * ROW_WORDS
            gate_ref[k:k + 1, :] = ex[r] / denom


def _route(h2, wq, keys):
    n_tok, d = h2.shape
    tt = ROUTE_TT
    return pl.pallas_call(
        _route_kernel,
        out_shape=(jax.ShapeDtypeStruct((N_PAIRS, n_tok), jnp.int32),
                   jax.ShapeDtypeStruct((N_PAIRS, n_tok), _F32)),
        grid=(n_tok // tt,),
        in_specs=[pl.BlockSpec((tt, d), lambda i: (i, 0)), _resident(), _resident()],
        out_specs=(pl.BlockSpec((N_PAIRS, tt), lambda i: (0, i)),
                   pl.BlockSpec((N_PAIRS, tt), lambda i: (0, i))),
        scratch_shapes=[
            pltpu.VMEM((2 * PEER_HEADS, PEER_TOPK, tt), _F32),
            pltpu.VMEM((2 * PEER_HEADS, PEER_TOPK, tt), jnp.int32),
        ],
        compiler_params=pltpu.CompilerParams(dimension_semantics=("arbitrary",)),
        name="peer_route",
    )(h2, wq, keys)


def _pack_table(w):
    n = w.shape[0]
    bits = lax.bitcast_convert_type(w.astype(_BF16), jnp.uint16).astype(jnp.uint32)
    words = bits[:, :HALF_D] | (bits[:, HALF_D:] << 16)
    return lax.bitcast_convert_type(words, jnp.int32).reshape(n * ROW_WORDS, 128)


def _gather_rows(idx_ref, tokens, tbl_ref, slab_tiles, lo=0, hi=N_PAIRS):
    rows = [idx_ref.at[t] for t in tokens]
    for k in range(lo, hi):
        for row, tile in zip(rows, slab_tiles):
            i = pl.multiple_of(row[k], ROW_WORDS)
            tile[pl.ds(ROW_WORDS * k, ROW_WORDS), :] = tbl_ref[pl.ds(i, ROW_WORDS), :]


def _rhs_half(tile, h):
    parts = [pltpu.bitcast(tile[pl.ds(2 * h + r, N_PAIRS, stride=ROW_WORDS), :], _BF16)
             for r in range(2)]
    return jnp.concatenate(parts, axis=1)


def _mxu_accumulate(acc, lhs_halves, tile, reg, transpose):
    for h in range(2):
        pltpu.matmul_push_rhs(_rhs_half(tile, h), staging_register=reg, mxu_index=h, transpose=transpose)
        pltpu.matmul_acc_lhs(acc, lhs_halves[h], mxu_index=h, load_staged_rhs=reg)


def _mxu_pop(acc):
    return [pltpu.matmul_pop(acc, (MXU_LHS_ROWS, 256), _F32, mxu_index=h) for h in range(2)]


def _slab_lhs(rows8):
    return jnp.concatenate([rows8, jnp.zeros_like(rows8)], axis=0).astype(_BF16)


def _row_masks():
    sub = lax.broadcasted_iota(jnp.int32, (8, 2 * N_PAIRS), 0)
    return [((sub & 3) == n).astype(_F32) for n in range(GATHER_NT)]


def _parity_mask():
    lane = lax.broadcasted_iota(jnp.int32, (8, 2 * N_PAIRS), 1)
    sub = lax.broadcasted_iota(jnp.int32, (8, 2 * N_PAIRS), 0)
    return ((sub >> 2) == (lane & 1)).astype(_F32)


def _pipelined_tokens(idx_ref, tbl_ref, tiles, accumulate, drain):
    tb = idx_ref.shape[0]
    n_stage = tb // GATHER_NT
    assert n_stage % 2 == 0
    set_a, set_b = tiles[:GATHER_NT], tiles[GATHER_NT:]
    half = N_PAIRS // 2
    slab_tokens = lambda j: [n * n_stage + j for n in range(GATHER_NT)]

    def half_stage(trip, first, cur, nxt, acc, acc_prev):
        j = lax.shift_right_logical(trip, 1)
        tokens = slab_tokens(j)
        if not first:
            drain(slab_tokens(jnp.maximum(j - 1, 0)), acc_prev)
        for n in ((0, 1) if first else (2, 3)):
            accumulate(tokens[n], n, cur[n], acc, n % 2)
        lo = 0 if first else half
        _gather_rows(idx_ref, slab_tokens(jnp.minimum(j + 1, n_stage - 1)), tbl_ref, nxt, lo, lo + half)

    def body(trip, carry):
        variants = ((True, set_a, set_b, 0), (False, set_a, set_b, 0),
                    (True, set_b, set_a, 1), (False, set_b, set_a, 1))
        for q, (first, cur, nxt, p) in enumerate(variants):
            @pl.when((trip & 3) == q)
            def _():
                half_stage(trip, first, cur, nxt, MXU_ACC[p], MXU_ACC[1 - p])

        return carry

    @pl.when(pl.program_id(0) == 0)
    def _():
        for acc in MXU_ACC:
            _mxu_pop(acc)

    _gather_rows(idx_ref, slab_tokens(0), tbl_ref, set_a)
    lax.fori_loop(0, 2 * n_stage, body, 0)
    drain(slab_tokens(n_stage - 1), MXU_ACC[(n_stage - 1) % 2])


def _u_pass_kernel(idx_ref, x_ref, gate_ref, tbl_ref, out_ref, *scratch):
    tiles, ybuf = scratch[:-1], scratch[-1]
    row_masks = _row_masks()
    parity = _parity_mask()
    upper = lax.broadcasted_iota(jnp.int32, (8, HALF_D), 0) >= GATHER_NT

    def accumulate(t, n, tile, acc, reg):
        xrow = x_ref[pl.ds(t, 1), :]
        x8 = jnp.where(upper, jnp.broadcast_to(xrow[:, HALF_D:], (8, HALF_D)),
                       jnp.broadcast_to(xrow[:, :HALF_D], (8, HALF_D)))
        lhs = _slab_lhs(x8 * row_masks[n][:, :1])
        _mxu_accumulate(acc, [lhs[:, :256], lhs[:, 256:]], tile, reg, transpose=True)

    def drain(tokens, acc):
        z0, z1 = _mxu_pop(acc)
        z = (z0[:8] + z1[:8]) * parity
        y4 = z[:GATHER_NT] + z[GATHER_NT:]
        for n, t in enumerate(tokens):
            ybuf[pl.ds(t, 1), :] = y4[n:n + 1, :]

    _pipelined_tokens(idx_ref, tbl_ref, tiles, accumulate, drain)
    y = ybuf[...]
    lane = lax.broadcasted_iota(jnp.int32, y.shape, 1)
    other = jnp.where((lane & 1) == 0, pltpu.roll(y, 2 * N_PAIRS - 1, axis=1), pltpu.roll(y, 1, axis=1))
    out_ref[...] = gate_ref[...] * _gelu(y + other)


def _v_pass_kernel(idx_ref, w_ref, tbl_ref, out_ref, *tiles):
    row_masks = _row_masks()
    parity = _parity_mask()

    def accumulate(t, n, tile, acc, reg):
        wrow = jnp.broadcast_to(w_ref[pl.ds(t, 1), :], (8, 2 * N_PAIRS))
        lhs = _slab_lhs(wrow * (row_masks[n] * parity))
        _mxu_accumulate(acc, [lhs, lhs], tile, reg, transpose=False)

    def drain(tokens, acc):
        for h, r in enumerate(_mxu_pop(acc)):
            for b in range(2):
                for n, t in enumerate(tokens):
                    row = GATHER_NT * b + n
                    out_ref[pl.ds(t, 1), pl.ds(HALF_D * b + 256 * h, 256)] = r[row:row + 1, :]

    _pipelined_tokens(idx_ref, tbl_ref, tiles, accumulate, drain)


def _gather_call(kernel_fn, name, n_tok, in_specs, out_spec, out_shape, n_extra_scratch, args):
    tb = GATHER_TB
    tile = pltpu.VMEM((ROW_WORDS * N_PAIRS, 128), jnp.int32)
    return pl.pallas_call(
        kernel_fn,
        out_shape=out_shape,
        grid=(n_tok // tb,),
        in_specs=in_specs,
        out_specs=out_spec,
        scratch_shapes=[tile] * (2 * GATHER_NT) + n_extra_scratch,
        compiler_params=pltpu.CompilerParams(
            dimension_semantics=("arbitrary",),
            vmem_limit_bytes=GATHER_VMEM_BYTES,
        ),
        name=name,
    )(*args)


def _u_pass(idx, h2, gate2, tbl):
    n_tok = idx.shape[0]
    tb = GATHER_TB
    return _gather_call(
        _u_pass_kernel, "peer_u_pass", n_tok,
        [pl.BlockSpec((tb, N_PAIRS), lambda i: (i, 0), memory_space=pltpu.SMEM),
         pl.BlockSpec((tb, D_MODEL), lambda i: (i, 0)),
         pl.BlockSpec((tb, 2 * N_PAIRS), lambda i: (i, 0)),
         _resident()],
        pl.BlockSpec((tb, 2 * N_PAIRS), lambda i: (i, 0)),
        jax.ShapeDtypeStruct((n_tok, 2 * N_PAIRS), _F32),
        [pltpu.VMEM((tb, 2 * N_PAIRS), _F32)],
        (idx, h2, gate2, tbl))


def _v_pass(idx, w2, tbl):
    n_tok = idx.shape[0]
    tb = GATHER_TB
    return _gather_call(
        _v_pass_kernel, "peer_v_pass", n_tok,
        [pl.BlockSpec((tb, N_PAIRS), lambda i: (i, 0), memory_space=pltpu.SMEM),
         pl.BlockSpec((tb, 2 * N_PAIRS), lambda i: (i, 0)),
         _resident()],
        pl.BlockSpec((tb, D_MODEL), lambda i: (i, 0)),
        jax.ShapeDtypeStruct((n_tok, D_MODEL), _F32),
        [],
        (idx, w2, tbl))


def _final_kernel(x1_ref, ffn_ref, mod_ref, g_ref, b_ref, o_ref):
    g2 = mod_ref[0][5:6]
    o_ref[0] = _ln(DEEPNORM_ALPHA * x1_ref[0] + g2 * ffn_ref[0]) * g_ref[...] + b_ref[...]


def _final_norm(x1, ffn, mod3, g, b):
    bsz, seq, d = x1.shape
    ts = NORM_TS
    blk = pl.BlockSpec((1, ts, d), lambda i, s: (i, s, 0))
    vec = pl.BlockSpec((1, d), lambda i, s: (0, 0))
    return pl.pallas_call(
        _final_kernel,
        out_shape=jax.ShapeDtypeStruct((bsz, seq, d), _F32),
        grid=(bsz, seq // ts),
        in_specs=[blk, blk, pl.BlockSpec((1, 6, d), lambda i, s: (i, 0, 0)), vec, vec],
        out_specs=blk,
        compiler_params=pltpu.CompilerParams(dimension_semantics=("arbitrary", "arbitrary")),
        name="final_norm",
    )(x1, ffn, mod3, g, b)


def kernel(x, c, w_cond, b_cond, w_in, gmlp_ln_g, gmlp_ln_b, w_spatial, b_spatial, conv_w, p_a, p_b, w_o, ln1_g, ln1_b, w_q_peer, sub_keys, expert_u, expert_v, ln2_g, ln2_b):
    bsz, seq, d = x.shape
    n_tok = bsz * seq
    depth = w_cond.shape[0]
    row = lambda v: v.reshape(1, d)
    for l in range(depth):
        mod3 = _cond_proj(c, w_cond[l], b_cond[l]).reshape(bsz, 6, d)
        x1, h2 = _mixer(
            x, mod3, w_in[l].astype(_BF16), row(gmlp_ln_g[l]), row(gmlp_ln_b[l]),
            w_spatial[l].astype(_BF16), b_spatial[l][:, :, None], conv_w[l],
            p_a[l].astype(_BF16), p_b[l].astype(_BF16), w_o[l].astype(_BF16),
            row(ln1_g[l]), row(ln1_b[l]))
        h2 = h2.reshape(n_tok, d)
        keys = sub_keys[l].astype(_BF16).reshape(2 * PEER_HEADS, PEER_N_KEYS, PEER_HALF)
        idx_t, gate_t = _route(h2, w_q_peer[l].astype(_BF16), keys)
        idx = idx_t.T
        gate2 = jnp.repeat(gate_t.T, 2, axis=1)
        w2 = _u_pass(idx, h2, gate2, _pack_table(expert_u[l]))
        ffn = _v_pass(idx, w2, _pack_table(expert_v[l])).reshape(bsz, seq, d)
        x = _final_norm(x1, ffn, mod3, row(ln2_g[l]), row(ln2_b[l]))
    return x
```

```python
import functools

import jax
import jax.numpy as jnp
from jax import lax
from jax.experimental import pallas as pl
from jax.experimental.pallas import tpu as pltpu

D_MODEL = 1024
CHUNK = 64
GMLP_BLOCK = 128
GMLP_GROUPS = 8
CONV_K = 3
PEER_HEADS = 8
PEER_HALF = 128
PEER_N_KEYS = 128
PEER_TOPK = 16
N_PAIRS = PEER_HEADS * PEER_TOPK
DEEPNORM_ALPHA = 2.0 ** 0.25
LN_EPS = 1e-5

ROW_WORDS = 4
HALF_D = D_MODEL // 2
MIX_TS = 512
ROUTE_TT = 256
GATHER_TB = 256
GATHER_NT = 4
NORM_TS = 512
MXU_LHS_ROWS = 16
MXU_ACC = (0, 4)
GATHER_VMEM_BYTES = 48 * 1024 * 1024
MIXER_VMEM_BYTES = 56 * 1024 * 1024

_F32 = jnp.float32
_BF16 = jnp.bfloat16
_NEG_INF = float("-inf")


def _ln(x):
    mu = jnp.mean(x, axis=-1, keepdims=True)
    xc = x - mu
    var = jnp.mean(xc * xc, axis=-1, keepdims=True)
    return xc * lax.rsqrt(var + LN_EPS)


def _gelu(x):
    return 0.5 * x * (1.0 + jnp.tanh(0.7978845608028654 * (x + 0.044715 * (x * x * x))))


def _sigmoid(x):
    return 1.0 / (1.0 + jnp.exp(-x))


def _resident():
    return pl.BlockSpec(memory_space=pltpu.VMEM)


def _cond_kernel(c_ref, w_ref, b_ref, o_ref):
    c = c_ref[...]
    a = (c * _sigmoid(c)).astype(_BF16)
    o_ref[...] = jnp.dot(a, w_ref[...].astype(_BF16), preferred_element_type=_F32) + b_ref[...]


def _cond_proj(c, w, b):
    bsz, d = c.shape
    n = w.shape[1]
    tn = 1024
    return pl.pallas_call(
        _cond_kernel,
        out_shape=jax.ShapeDtypeStruct((bsz, n), _F32),
        grid=(n // tn,),
        in_specs=[
            pl.BlockSpec((bsz, d), lambda j: (0, 0)),
            pl.BlockSpec((d, tn), lambda j: (0, j)),
            pl.BlockSpec((1, tn), lambda j: (0, j)),
        ],
        out_specs=pl.BlockSpec((bsz, tn), lambda j: (0, j)),
        compiler_params=pltpu.CompilerParams(dimension_semantics=("arbitrary",)),
        name="cond_proj",
    )(c, w, b.reshape(1, n))


def _mixer_kernel(x_ref, mod_ref, w_in_ref, gg_ref, gb_ref, ws_ref, bs_ref, cw_ref, pa_ref, pb_ref,
                  wo_ref, l1g_ref, l1b_ref, x1_ref, h2_ref, prev_ref, gu_ref, vn_ref, ya_ref):
    ts = x_ref.shape[1]
    d = D_MODEL

    @pl.when(pl.program_id(1) == 0)
    def _():
        prev_ref[...] = jnp.zeros_like(prev_ref)

    x = x_ref[0]
    mod = mod_ref[0]
    sh1, sc1, g1, sh2, sc2 = (mod[i:i + 1] for i in range(5))
    h = (_ln(x) * (1.0 + sc1) + sh1).astype(_BF16)

    def proj(j):
        return jnp.dot(h, w_in_ref[:, j * d:(j + 1) * d], preferred_element_type=_F32)

    gu_ref[...] = _gelu(proj(0))
    vn_ref[...] = (_ln(_gelu(proj(1))) * gg_ref[...] + gb_ref[...]).astype(_BF16)
    qi = lax.broadcasted_iota(jnp.int32, (GMLP_BLOCK, GMLP_BLOCK), 0) // CHUNK
    kj = lax.broadcasted_iota(jnp.int32, (GMLP_BLOCK, GMLP_BLOCK), 1) // CHUNK
    causal = kj <= qi
    for g in range(GMLP_GROUPS):
        wg = jnp.where(causal, ws_ref[g], jnp.zeros((), _BF16))
        cols = slice(g * GMLP_BLOCK, (g + 1) * GMLP_BLOCK)
        for n in range(ts // GMLP_BLOCK):
            rows = slice(n * GMLP_BLOCK, (n + 1) * GMLP_BLOCK)
            mixed = jnp.dot(wg, vn_ref[rows, cols], preferred_element_type=_F32) + bs_ref[g]
            ya_ref[rows, cols] = (gu_ref[rows, cols] * mixed).astype(_BF16)

    g_b = proj(2)
    zc = proj(3) * proj(4)
    ext = jnp.concatenate([prev_ref[...], zc], axis=0)
    prev_ref[...] = zc[ts - 8:, :]
    cw = cw_ref[...]
    y = ext[6:ts + 6, :] * cw[0:1] + ext[7:ts + 7, :] * cw[1:2] + zc * cw[2:3]
    y_b = (g_b * y).astype(_BF16)

    merged = (_sigmoid(proj(5)) * jnp.dot(ya_ref[...], pa_ref[...], preferred_element_type=_F32)
              + _sigmoid(proj(6)) * jnp.dot(y_b, pb_ref[...], preferred_element_type=_F32))
    mix = jnp.dot(merged.astype(_BF16), wo_ref[...], preferred_element_type=_F32)
    x1 = _ln(DEEPNORM_ALPHA * x + g1 * mix) * l1g_ref[...] + l1b_ref[...]
    x1_ref[0] = x1
    h2_ref[0] = _ln(x1) * (1.0 + sc2) + sh2


def _mixer(x, mod3, w_in, gg, gb, ws, bs, cw, pa, pb, wo, l1g, l1b):
    bsz, seq, d = x.shape
    ts = MIX_TS
    blk = pl.BlockSpec((1, ts, d), lambda b, s: (b, s, 0))
    return pl.pallas_call(
        _mixer_kernel,
        out_shape=(jax.ShapeDtypeStruct((bsz, seq, d), _F32),
                   jax.ShapeDtypeStruct((bsz, seq, d), _F32)),
        grid=(bsz, seq // ts),
        in_specs=[blk, pl.BlockSpec((1, 6, d), lambda b, s: (b, 0, 0))] + [_resident()] * 11,
        out_specs=(blk, blk),
        scratch_shapes=[
            pltpu.VMEM((8, d), _F32),
            pltpu.VMEM((ts, d), _F32),
            pltpu.VMEM((ts, d), _BF16),
            pltpu.VMEM((ts, d), _BF16),
        ],
        compiler_params=pltpu.CompilerParams(
            dimension_semantics=("arbitrary", "arbitrary"),
            vmem_limit_bytes=MIXER_VMEM_BYTES,
        ),
        name="mixer",
    )(x, mod3, w_in, gg, gb, ws, bs, cw, pa, pb, wo, l1g, l1b)


def _sorted_top(vals, ids, payload, k):
    vals, ids = list(vals), list(ids)
    pay = None if payload is None else list(payload)
    g = len(vals)
    for phase in range(g):
        for i in range(phase % 2, g - 1, 2):
            swap = vals[i + 1] > vals[i]
            vals[i], vals[i + 1] = jnp.maximum(vals[i], vals[i + 1]), jnp.minimum(vals[i], vals[i + 1])
            ids[i], ids[i + 1] = jnp.where(swap, ids[i + 1], ids[i]), jnp.where(swap, ids[i], ids[i + 1])
            if pay is not None:
                pay[i], pay[i + 1] = jnp.where(swap, pay[i + 1], pay[i]), jnp.where(swap, pay[i], pay[i + 1])
    big = jnp.int32(2 ** 30)
    top_v, top_p = [], []
    for r in range(k):
        m = jnp.max(vals[0], axis=0, keepdims=True)
        sel = jnp.min(jnp.where(vals[0] == m, ids[0], big), axis=0, keepdims=True)
        hit = ids[0] == sel
        top_v.append(m)
        top_p.append(sel if pay is None else jnp.max(jnp.where(hit, pay[0], -1), axis=0, keepdims=True))
        for i in range(min(g, k - r - 1)):
            nxt = i + 1 < g
            vals[i] = jnp.where(hit, vals[i + 1] if nxt else _NEG_INF, vals[i])
            ids[i] = jnp.where(hit, ids[i + 1] if nxt else big, ids[i])
            if pay is not None and nxt:
                pay[i] = jnp.where(hit, pay[i + 1], pay[i])
    return top_v, top_p


def _route_kernel(h_ref, wq_ref, keys_ref, idx_ref, gate_ref, st_ref, it_ref, idx_t_ref, gate_t_ref):
    tt = h_ref.shape[0]
    lanes = 128
    q = jnp.dot(h_ref[...].astype(_BF16), wq_ref[...], preferred_element_type=_F32).astype(_BF16)
    row = lax.broadcasted_iota(jnp.int32, (8, lanes), 0)
    groups = PEER_N_KEYS // 8
    for hp in range(2 * PEER_HEADS):
        s = lax.dot_general(keys_ref[hp], q[:, hp * PEER_HALF:(hp + 1) * PEER_HALF],
                            (((1,), (1,)), ((), ())), preferred_element_type=_F32)
        for c in range(tt // lanes):
            cols = slice(c * lanes, (c + 1) * lanes)
            top_v, top_i = _sorted_top([s[8 * i:8 * i + 8, cols] for i in range(groups)],
                                       [row + 8 * i for i in range(groups)], None, PEER_TOPK)
            for r in range(PEER_TOPK):
                st_ref[hp, r:r + 1, cols] = top_v[r]
                it_ref[hp, r:r + 1, cols] = top_i[r]

    for hd in range(PEER_HEADS):
        for c in range(tt // lanes):
            cols = slice(c * lanes, (c + 1) * lanes)
            s1, s2 = st_ref[2 * hd, :, cols], st_ref[2 * hd + 1, :, cols]
            i1, i2 = it_ref[2 * hd, :, cols], it_ref[2 * hd + 1, :, cols]
            vals, eids, cids = [], [], []
            for a in range(8):
                vals.append(s1[a:a + 1] + s2[0:8])
                eids.append(i1[a:a + 1] * PEER_N_KEYS + i2[0:8])
                cids.append(a * PEER_TOPK + row)
                if a == 0:
                    vals.append(s1[0:1] + s2[8:16])
                    eids.append(i1[0:1] * PEER_N_KEYS + i2[8:16])
                    cids.append(8 + row)
            vals.append(s1[8:16] + s2[0:1])
            eids.append(i1[8:16] * PEER_N_KEYS + i2[0:1])
            cids.append((8 + row) * PEER_TOPK)
            top_s, top_e = _sorted_top(vals, cids, eids, PEER_TOPK)
            ex = [jnp.exp(v - top_s[0]) for v in top_s]
            denom = functools.reduce(lambda p, r: p + r, ex)
            for r in range(PEER_TOPK):
                k = hd * PEER_TOPK + r
                idx_t_ref[k:k + 1, cols] = top_e[r] * ROW_WORDS
                gate_t_ref[2 * k:2 * k + 2, cols] = jnp.broadcast_to(ex[r] / denom, (2, lanes))
    idx_ref[...] = idx_t_ref[...].T
    gate_ref[...] = gate_t_ref[...].T


def _route(h2, wq, keys):
    n_tok, d = h2.shape
    tt = ROUTE_TT
    return pl.pallas_call(
        _route_kernel,
        out_shape=(jax.ShapeDtypeStruct((n_tok, N_PAIRS), jnp.int32),
                   jax.ShapeDtypeStruct((n_tok, 2 * N_PAIRS), _F32)),
        grid=(n_tok // tt,),
        in_specs=[pl.BlockSpec((tt, d), lambda i: (i, 0)), _resident(), _resident()],
        out_specs=(pl.BlockSpec((tt, N_PAIRS), lambda i: (i, 0)),
                   pl.BlockSpec((tt, 2 * N_PAIRS), lambda i: (i, 0))),
        scratch_shapes=[
            pltpu.VMEM((2 * PEER_HEADS, PEER_TOPK, tt), _F32),
            pltpu.VMEM((2 * PEER_HEADS, PEER_TOPK, tt), jnp.int32),
            pltpu.VMEM((N_PAIRS, tt), jnp.int32),
            pltpu.VMEM((2 * N_PAIRS, tt), _F32),
        ],
        compiler_params=pltpu.CompilerParams(dimension_semantics=("arbitrary",)),
        name="peer_route",
    )(h2, wq, keys)


def _pack_table(w):
    n = w.shape[0]
    bits = lax.bitcast_convert_type(w.astype(_BF16), jnp.uint16).astype(jnp.uint32)
    words = bits[:, :HALF_D] | (bits[:, HALF_D:] << 16)
    return lax.bitcast_convert_type(words, jnp.int32).reshape(n * ROW_WORDS, 128)


def _gather_rows(idx_ref, tokens, tbl_ref, slab_tiles, lo=0, hi=N_PAIRS):
    rows = [idx_ref.at[t] for t in tokens]
    for k in range(lo, hi):
        for row, tile in zip(rows, slab_tiles):
            i = pl.multiple_of(row[k], ROW_WORDS)
            tile[pl.ds(ROW_WORDS * k, ROW_WORDS), :] = tbl_ref[pl.ds(i, ROW_WORDS), :]


def _rhs_half(tile, h):
    parts = [pltpu.bitcast(tile[pl.ds(2 * h + r, N_PAIRS, stride=ROW_WORDS), :], _BF16)
             for r in range(2)]
    return jnp.concatenate(parts, axis=1)


def _mxu_accumulate(acc, lhs_halves, tile, reg, transpose):
    def step(h):
        pltpu.matmul_push_rhs(_rhs_half(tile, h), staging_register=reg, mxu_index=h, transpose=transpose)
        pltpu.matmul_acc_lhs(acc, lhs_halves[h], mxu_index=h, load_staged_rhs=reg)

    return [functools.partial(step, h) for h in range(2)]


def _mxu_pop(acc):
    return [pltpu.matmul_pop(acc, (MXU_LHS_ROWS, 256), _F32, mxu_index=h) for h in range(2)]


def _slab_lhs(rows8):
    return jnp.concatenate([rows8, jnp.zeros_like(rows8)], axis=0).astype(_BF16)


def _row_masks():
    sub = lax.broadcasted_iota(jnp.int32, (8, 2 * N_PAIRS), 0)
    return [((sub & 3) == n).astype(_F32) for n in range(GATHER_NT)]


def _parity_mask():
    lane = lax.broadcasted_iota(jnp.int32, (8, 2 * N_PAIRS), 1)
    sub = lax.broadcasted_iota(jnp.int32, (8, 2 * N_PAIRS), 0)
    return ((sub >> 2) == (lane & 1)).astype(_F32)


def _pipelined_tokens(idx_ref, tbl_ref, tiles, accumulate, drain):
    tb = idx_ref.shape[0]
    n_stage = tb // GATHER_NT
    assert n_stage % 2 == 0
    set_a, set_b = tiles[:GATHER_NT], tiles[GATHER_NT:]
    half = N_PAIRS // 2
    slab_tokens = lambda j: [n * n_stage + j for n in range(GATHER_NT)]

    def half_stage(trip, first, cur, nxt, acc, acc_prev):
        j = lax.shift_right_logical(trip, 1)
        tokens = slab_tokens(j)
        if not first:
            drain(slab_tokens(jnp.maximum(j - 1, 0)), acc_prev)
        steps = []
        for n in ((0, 1) if first else (2, 3)):
            steps += accumulate(tokens[n], n, cur[n], acc, n % 2)
        ahead = slab_tokens(jnp.minimum(j + 1, n_stage - 1))
        lo = 0 if first else half
        part = half // len(steps)
        for i, step in enumerate(steps):
            step()
            _gather_rows(idx_ref, ahead, tbl_ref, nxt, lo + part * i, lo + part * (i + 1))

    def body(trip, carry):
        variants = ((True, set_a, set_b, 0), (False, set_a, set_b, 0),
                    (True, set_b, set_a, 1), (False, set_b, set_a, 1))
        for q, (first, cur, nxt, p) in enumerate(variants):
            @pl.when((trip & 3) == q)
            def _():
                half_stage(trip, first, cur, nxt, MXU_ACC[p], MXU_ACC[1 - p])

        return carry

    @pl.when(pl.program_id(0) == 0)
    def _():
        for acc in MXU_ACC:
            _mxu_pop(acc)

    _gather_rows(idx_ref, slab_tokens(0), tbl_ref, set_a)
    lax.fori_loop(0, 2 * n_stage, body, 0)
    drain(slab_tokens(n_stage - 1), MXU_ACC[(n_stage - 1) % 2])


def _u_pass_kernel(idx_ref, x_ref, gate_ref, tbl_ref, out_ref, *scratch):
    tiles, ybuf = scratch[:-1], scratch[-1]
    row_masks = _row_masks()
    parity = _parity_mask()
    upper = lax.broadcasted_iota(jnp.int32, (8, HALF_D), 0) >= GATHER_NT

    def accumulate(t, n, tile, acc, reg):
        xrow = x_ref[pl.ds(t, 1), :]
        x8 = jnp.where(upper, jnp.broadcast_to(xrow[:, HALF_D:], (8, HALF_D)),
                       jnp.broadcast_to(xrow[:, :HALF_D], (8, HALF_D)))
        lhs = _slab_lhs(x8 * row_masks[n][:, :1])
        return _mxu_accumulate(acc, [lhs[:, :256], lhs[:, 256:]], tile, reg, transpose=True)

    def drain(tokens, acc):
        z0, z1 = _mxu_pop(acc)
        z = (z0[:8] + z1[:8]) * parity
        y4 = z[:GATHER_NT] + z[GATHER_NT:]
        for n, t in enumerate(tokens):
            ybuf[pl.ds(t, 1), :] = y4[n:n + 1, :]

    _pipelined_tokens(idx_ref, tbl_ref, tiles, accumulate, drain)
    y = ybuf[...]
    lane = lax.broadcasted_iota(jnp.int32, y.shape, 1)
    other = jnp.where((lane & 1) == 0, pltpu.roll(y, 2 * N_PAIRS - 1, axis=1), pltpu.roll(y, 1, axis=1))
    out_ref[...] = gate_ref[...] * _gelu(y + other)


def _v_pass_kernel(idx_ref, w_ref, tbl_ref, out_ref, *tiles):
    row_masks = _row_masks()
    parity = _parity_mask()

    def accumulate(t, n, tile, acc, reg):
        wrow = jnp.broadcast_to(w_ref[pl.ds(t, 1), :], (8, 2 * N_PAIRS))
        lhs = _slab_lhs(wrow * (row_masks[n] * parity))
        return _mxu_accumulate(acc, [lhs, lhs], tile, reg, transpose=False)

    def drain(tokens, acc):
        for h, r in enumerate(_mxu_pop(acc)):
            for b in range(2):
                for n, t in enumerate(tokens):
                    row = GATHER_NT * b + n
                    out_ref[pl.ds(t, 1), pl.ds(HALF_D * b + 256 * h, 256)] = r[row:row + 1, :]

    _pipelined_tokens(idx_ref, tbl_ref, tiles, accumulate, drain)


def _gather_call(kernel_fn, name, n_tok, in_specs, out_spec, out_shape, n_extra_scratch, args):
    tb = GATHER_TB
    tile = pltpu.VMEM((ROW_WORDS * N_PAIRS, 128), jnp.int32)
    return pl.pallas_call(
        kernel_fn,
        out_shape=out_shape,
        grid=(n_tok // tb,),
        in_specs=in_specs,
        out_specs=out_spec,
        scratch_shapes=[tile] * (2 * GATHER_NT) + n_extra_scratch,
        compiler_params=pltpu.CompilerParams(
            dimension_semantics=("arbitrary",),
            vmem_limit_bytes=GATHER_VMEM_BYTES,
        ),
        name=name,
    )(*args)


def _u_pass(idx, h2, gate2, tbl):
    n_tok = idx.shape[0]
    tb = GATHER_TB
    return _gather_call(
        _u_pass_kernel, "peer_u_pass", n_tok,
        [pl.BlockSpec((tb, N_PAIRS), lambda i: (i, 0), memory_space=pltpu.SMEM),
         pl.BlockSpec((tb, D_MODEL), lambda i: (i, 0)),
         pl.BlockSpec((tb, 2 * N_PAIRS), lambda i: (i, 0)),
         _resident()],
        pl.BlockSpec((tb, 2 * N_PAIRS), lambda i: (i, 0)),
        jax.ShapeDtypeStruct((n_tok, 2 * N_PAIRS), _F32),
        [pltpu.VMEM((tb, 2 * N_PAIRS), _F32)],
        (idx, h2, gate2, tbl))


def _v_pass(idx, w2, tbl):
    n_tok = idx.shape[0]
    tb = GATHER_TB
    return _gather_call(
        _v_pass_kernel, "peer_v_pass", n_tok,
        [pl.BlockSpec((tb, N_PAIRS), lambda i: (i, 0), memory_space=pltpu.SMEM),
         pl.BlockSpec((tb, 2 * N_PAIRS), lambda i: (i, 0)),
         _resident()],
        pl.BlockSpec((tb, D_MODEL), lambda i: (i, 0)),
        jax.ShapeDtypeStruct((n_tok, D_MODEL), _F32),
        [],
        (idx, w2, tbl))


def _final_kernel(x1_ref, ffn_ref, mod_ref, g_ref, b_ref, o_ref):
    g2 = mod_ref[0][5:6]
    o_ref[0] = _ln(DEEPNORM_ALPHA * x1_ref[0] + g2 * ffn_ref[0]) * g_ref[...] + b_ref[...]


def _final_norm(x1, ffn, mod3, g, b):
    bsz, seq, d = x1.shape
    ts = NORM_TS
    blk = pl.BlockSpec((1, ts, d), lambda i, s: (i, s, 0))
    vec = pl.BlockSpec((1, d), lambda i, s: (0, 0))
    return pl.pallas_call(
        _final_kernel,
        out_shape=jax.ShapeDtypeStruct((bsz, seq, d), _F32),
        grid=(bsz, seq // ts),
        in_specs=[blk, blk, pl.BlockSpec((1, 6, d), lambda i, s: (i, 0, 0)), vec, vec],
        out_specs=blk,
        compiler_params=pltpu.CompilerParams(dimension_semantics=("arbitrary", "arbitrary")),
        name="final_norm",
    )(x1, ffn, mod3, g, b)


def kernel(x, c, w_cond, b_cond, w_in, gmlp_ln_g, gmlp_ln_b, w_spatial, b_spatial, conv_w, p_a, p_b, w_o, ln1_g, ln1_b, w_q_peer, sub_keys, expert_u, expert_v, ln2_g, ln2_b):
    bsz, seq, d = x.shape
    n_tok = bsz * seq
    depth = w_cond.shape[0]
    row = lambda v: v.reshape(1, d)
    for l in range(depth):
        mod3 = _cond_proj(c, w_cond[l], b_cond[l]).reshape(bsz, 6, d)
        x1, h2 = _mixer(
            x, mod3, w_in[l].astype(_BF16), row(gmlp_ln_g[l]), row(gmlp_ln_b[l]),
            w_spatial[l].astype(_BF16), b_spatial[l][:, :, None], conv_w[l],
            p_a[l].astype(_BF16), p_b[l].astype(_BF16), w_o[l].astype(_BF16),
            row(ln1_g[l]), row(ln1_b[l]))
        h2 = h2.reshape(n_tok, d)
        keys = sub_keys[l].astype(_BF16).reshape(2 * PEER_HEADS, PEER_N_KEYS, PEER_HALF)
        idx, gate2 = _route(h2, w_q_peer[l].astype(_BF16), keys)
        w2 = _u_pass(idx, h2, gate2, _pack_table(expert_u[l]))
        ffn = _v_pass(idx, w2, _pack_table(expert_v[l])).reshape(bsz, seq, d)
        x = _final_norm(x1, ffn, mod3, row(ln2_g[l]), row(ln2_b[l]))
    return x
```

```python
import functools

import jax
import jax.numpy as jnp
from jax import lax
from jax.experimental import pallas as pl
from jax.experimental.pallas import tpu as pltpu

D_MODEL = 1024
CHUNK = 64
GMLP_BLOCK = 128
GMLP_GROUPS = 8
CONV_K = 3
PEER_HEADS = 8
PEER_HALF = 128
PEER_N_KEYS = 128
PEER_TOPK = 16
N_PAIRS = PEER_HEADS * PEER_TOPK
DEEPNORM_ALPHA = 2.0 ** 0.25
LN_EPS = 1e-5

ROW_WORDS = 4
HALF_D = D_MODEL // 2
MIX_TS = 512
ROUTE_TT = 256
GATHER_TB = 256
GATHER_NT = 4
NORM_TS = 512
MXU_LHS_ROWS = 16
MXU_ACC = (0, 4, 8)
GATHER_VMEM_BYTES = 48 * 1024 * 1024
MIXER_VMEM_BYTES = 56 * 1024 * 1024

_F32 = jnp.float32
_BF16 = jnp.bfloat16
_NEG_INF = float("-inf")


def _ln(x):
    mu = jnp.mean(x, axis=-1, keepdims=True)
    xc = x - mu
    var = jnp.mean(xc * xc, axis=-1, keepdims=True)
    return xc * lax.rsqrt(var + LN_EPS)


def _gelu(x):
    return 0.5 * x * (1.0 + jnp.tanh(0.7978845608028654 * (x + 0.044715 * (x * x * x))))


def _sigmoid(x):
    return 1.0 / (1.0 + jnp.exp(-x))


def _resident():
    return pl.BlockSpec(memory_space=pltpu.VMEM)


def _cond_kernel(c_ref, w_ref, b_ref, o_ref):
    c = c_ref[...]
    a = (c * _sigmoid(c)).astype(_BF16)
    o_ref[...] = jnp.dot(a, w_ref[...].astype(_BF16), preferred_element_type=_F32) + b_ref[...]


def _cond_proj(c, w, b):
    bsz, d = c.shape
    n = w.shape[1]
    tn = 1024
    return pl.pallas_call(
        _cond_kernel,
        out_shape=jax.ShapeDtypeStruct((bsz, n), _F32),
        grid=(n // tn,),
        in_specs=[
            pl.BlockSpec((bsz, d), lambda j: (0, 0)),
            pl.BlockSpec((d, tn), lambda j: (0, j)),
            pl.BlockSpec((1, tn), lambda j: (0, j)),
        ],
        out_specs=pl.BlockSpec((bsz, tn), lambda j: (0, j)),
        compiler_params=pltpu.CompilerParams(dimension_semantics=("arbitrary",)),
        name="cond_proj",
    )(c, w, b.reshape(1, n))


def _mixer_kernel(x_ref, mod_ref, w_in_ref, gg_ref, gb_ref, ws_ref, bs_ref, cw_ref, pa_ref, pb_ref,
                  wo_ref, l1g_ref, l1b_ref, x1_ref, h2_ref, prev_ref, gu_ref, vn_ref, ya_ref):
    ts = x_ref.shape[1]
    d = D_MODEL

    @pl.when(pl.program_id(1) == 0)
    def _():
        prev_ref[...] = jnp.zeros_like(prev_ref)

    x = x_ref[0]
    mod = mod_ref[0]
    sh1, sc1, g1, sh2, sc2 = (mod[i:i + 1] for i in range(5))
    h = (_ln(x) * (1.0 + sc1) + sh1).astype(_BF16)

    def proj(j):
        return jnp.dot(h, w_in_ref[:, j * d:(j + 1) * d], preferred_element_type=_F32)

    gu_ref[...] = _gelu(proj(0))
    vn_ref[...] = (_ln(_gelu(proj(1))) * gg_ref[...] + gb_ref[...]).astype(_BF16)
    qi = lax.broadcasted_iota(jnp.int32, (GMLP_BLOCK, GMLP_BLOCK), 0) // CHUNK
    kj = lax.broadcasted_iota(jnp.int32, (GMLP_BLOCK, GMLP_BLOCK), 1) // CHUNK
    causal = kj <= qi
    for g in range(GMLP_GROUPS):
        wg = jnp.where(causal, ws_ref[g], jnp.zeros((), _BF16))
        cols = slice(g * GMLP_BLOCK, (g + 1) * GMLP_BLOCK)
        for n in range(ts // GMLP_BLOCK):
            rows = slice(n * GMLP_BLOCK, (n + 1) * GMLP_BLOCK)
            mixed = jnp.dot(wg, vn_ref[rows, cols], preferred_element_type=_F32) + bs_ref[g]
            ya_ref[rows, cols] = (gu_ref[rows, cols] * mixed).astype(_BF16)

    g_b = proj(2)
    zc = proj(3) * proj(4)
    ext = jnp.concatenate([prev_ref[...], zc], axis=0)
    prev_ref[...] = zc[ts - 8:, :]
    cw = cw_ref[...]
    y = ext[6:ts + 6, :] * cw[0:1] + ext[7:ts + 7, :] * cw[1:2] + zc * cw[2:3]
    y_b = (g_b * y).astype(_BF16)

    merged = (_sigmoid(proj(5)) * jnp.dot(ya_ref[...], pa_ref[...], preferred_element_type=_F32)
              + _sigmoid(proj(6)) * jnp.dot(y_b, pb_ref[...], preferred_element_type=_F32))
    mix = jnp.dot(merged.astype(_BF16), wo_ref[...], preferred_element_type=_F32)
    x1 = _ln(DEEPNORM_ALPHA * x + g1 * mix) * l1g_ref[...] + l1b_ref[...]
    x1_ref[0] = x1
    h2_ref[0] = _ln(x1) * (1.0 + sc2) + sh2


def _mixer(x, mod3, w_in, gg, gb, ws, bs, cw, pa, pb, wo, l1g, l1b):
    bsz, seq, d = x.shape
    ts = MIX_TS
    blk = pl.BlockSpec((1, ts, d), lambda b, s: (b, s, 0))
    return pl.pallas_call(
        _mixer_kernel,
        out_shape=(jax.ShapeDtypeStruct((bsz, seq, d), _F32),
                   jax.ShapeDtypeStruct((bsz, seq, d), _F32)),
        grid=(bsz, seq // ts),
        in_specs=[blk, pl.BlockSpec((1, 6, d), lambda b, s: (b, 0, 0))] + [_resident()] * 11,
        out_specs=(blk, blk),
        scratch_shapes=[
            pltpu.VMEM((8, d), _F32),
            pltpu.VMEM((ts, d), _F32),
            pltpu.VMEM((ts, d), _BF16),
            pltpu.VMEM((ts, d), _BF16),
        ],
        compiler_params=pltpu.CompilerParams(
            dimension_semantics=("arbitrary", "arbitrary"),
            vmem_limit_bytes=MIXER_VMEM_BYTES,
        ),
        name="mixer",
    )(x, mod3, w_in, gg, gb, ws, bs, cw, pa, pb, wo, l1g, l1b)


def _sorted_top(vals, ids, payload, k):
    vals, ids = list(vals), list(ids)
    pay = None if payload is None else list(payload)
    g = len(vals)
    for phase in range(g):
        for i in range(phase % 2, g - 1, 2):
            swap = vals[i + 1] > vals[i]
            vals[i], vals[i + 1] = jnp.maximum(vals[i], vals[i + 1]), jnp.minimum(vals[i], vals[i + 1])
            ids[i], ids[i + 1] = jnp.where(swap, ids[i + 1], ids[i]), jnp.where(swap, ids[i], ids[i + 1])
            if pay is not None:
                pay[i], pay[i + 1] = jnp.where(swap, pay[i + 1], pay[i]), jnp.where(swap, pay[i], pay[i + 1])
    big = jnp.int32(2 ** 30)
    top_v, top_p = [], []
    for r in range(k):
        m = jnp.max(vals[0], axis=0, keepdims=True)
        sel = jnp.min(jnp.where(vals[0] == m, ids[0], big), axis=0, keepdims=True)
        hit = ids[0] == sel
        top_v.append(m)
        top_p.append(sel if pay is None else jnp.max(jnp.where(hit, pay[0], -1), axis=0, keepdims=True))
        for i in range(min(g, k - r - 1)):
            nxt = i + 1 < g
            vals[i] = jnp.where(hit, vals[i + 1] if nxt else _NEG_INF, vals[i])
            ids[i] = jnp.where(hit, ids[i + 1] if nxt else big, ids[i])
            if pay is not None and nxt:
                pay[i] = jnp.where(hit, pay[i + 1], pay[i])
    return top_v, top_p


def _route_kernel(h_ref, wq_ref, keys_ref, idx_ref, gate_ref, st_ref, it_ref, idx_t_ref, gate_t_ref):
    tt = h_ref.shape[0]
    lanes = 128
    q = jnp.dot(h_ref[...].astype(_BF16), wq_ref[...], preferred_element_type=_F32).astype(_BF16)
    row = lax.broadcasted_iota(jnp.int32, (8, lanes), 0)
    groups = PEER_N_KEYS // 8
    for hp in range(2 * PEER_HEADS):
        s = lax.dot_general(keys_ref[hp], q[:, hp * PEER_HALF:(hp + 1) * PEER_HALF],
                            (((1,), (1,)), ((), ())), preferred_element_type=_F32)
        for c in range(tt // lanes):
            cols = slice(c * lanes, (c + 1) * lanes)
            top_v, top_i = _sorted_top([s[8 * i:8 * i + 8, cols] for i in range(groups)],
                                       [row + 8 * i for i in range(groups)], None, PEER_TOPK)
            for r in range(PEER_TOPK):
                st_ref[hp, r:r + 1, cols] = top_v[r]
                it_ref[hp, r:r + 1, cols] = top_i[r]

    for hd in range(PEER_HEADS):
        for c in range(tt // lanes):
            cols = slice(c * lanes, (c + 1) * lanes)
            s1, s2 = st_ref[2 * hd, :, cols], st_ref[2 * hd + 1, :, cols]
            i1, i2 = it_ref[2 * hd, :, cols], it_ref[2 * hd + 1, :, cols]
            vals, eids, cids = [], [], []
            for a in range(8):
                vals.append(s1[a:a + 1] + s2[0:8])
                eids.append(i1[a:a + 1] * PEER_N_KEYS + i2[0:8])
                cids.append(a * PEER_TOPK + row)
                if a == 0:
                    vals.append(s1[0:1] + s2[8:16])
                    eids.append(i1[0:1] * PEER_N_KEYS + i2[8:16])
                    cids.append(8 + row)
            vals.append(s1[8:16] + s2[0:1])
            eids.append(i1[8:16] * PEER_N_KEYS + i2[0:1])
            cids.append((8 + row) * PEER_TOPK)
            top_s, top_e = _sorted_top(vals, cids, eids, PEER_TOPK)
            ex = [jnp.exp(v - top_s[0]) for v in top_s]
            denom = functools.reduce(lambda p, r: p + r, ex)
            for r in range(PEER_TOPK):
                k = hd * PEER_TOPK + r
                idx_t_ref[k:k + 1, cols] = top_e[r] * ROW_WORDS
                gate_t_ref[2 * k:2 * k + 2, cols] = jnp.broadcast_to(ex[r] / denom, (2, lanes))
    idx_ref[...] = idx_t_ref[...].T
    gate_ref[...] = gate_t_ref[...].T


def _route(h2, wq, keys):
    n_tok, d = h2.shape
    tt = ROUTE_TT
    return pl.pallas_call(
        _route_kernel,
        out_shape=(jax.ShapeDtypeStruct((n_tok, N_PAIRS), jnp.int32),
                   jax.ShapeDtypeStruct((n_tok, 2 * N_PAIRS), _F32)),
        grid=(n_tok // tt,),
        in_specs=[pl.BlockSpec((tt, d), lambda i: (i, 0)), _resident(), _resident()],
        out_specs=(pl.BlockSpec((tt, N_PAIRS), lambda i: (i, 0)),
                   pl.BlockSpec((tt, 2 * N_PAIRS), lambda i: (i, 0))),
        scratch_shapes=[
            pltpu.VMEM((2 * PEER_HEADS, PEER_TOPK, tt), _F32),
            pltpu.VMEM((2 * PEER_HEADS, PEER_TOPK, tt), jnp.int32),
            pltpu.VMEM((N_PAIRS, tt), jnp.int32),
            pltpu.VMEM((2 * N_PAIRS, tt), _F32),
        ],
        compiler_params=pltpu.CompilerParams(dimension_semantics=("arbitrary",)),
        name="peer_route",
    )(h2, wq, keys)


def _pack_table(w):
    n = w.shape[0]
    bits = lax.bitcast_convert_type(w.astype(_BF16), jnp.uint16).astype(jnp.uint32)
    words = bits[:, :HALF_D] | (bits[:, HALF_D:] << 16)
    return lax.bitcast_convert_type(words, jnp.int32).reshape(n * ROW_WORDS, 128)


def _gather_rows(idx_ref, tokens, tbl_ref, slab_tiles, lo=0, hi=N_PAIRS):
    rows = [idx_ref.at[t] for t in tokens]
    for k in range(lo, hi):
        for row, tile in zip(rows, slab_tiles):
            i = pl.multiple_of(row[k], ROW_WORDS)
            tile[pl.ds(ROW_WORDS * k, ROW_WORDS), :] = tbl_ref[pl.ds(i, ROW_WORDS), :]


def _rhs_half(tile, h):
    parts = [pltpu.bitcast(tile[pl.ds(2 * h + r, N_PAIRS, stride=ROW_WORDS), :], _BF16)
             for r in range(2)]
    return jnp.concatenate(parts, axis=1)


def _mxu_accumulate(acc, lhs_halves, tile, reg, transpose):
    def step(h):
        pltpu.matmul_push_rhs(_rhs_half(tile, h), staging_register=reg, mxu_index=h, transpose=transpose)
        pltpu.matmul_acc_lhs(acc, lhs_halves[h], mxu_index=h, load_staged_rhs=reg)

    return [functools.partial(step, h) for h in range(2)]


def _mxu_pop(acc):
    return [pltpu.matmul_pop(acc, (MXU_LHS_ROWS, 256), _F32, mxu_index=h) for h in range(2)]


def _slab_lhs(rows8):
    return jnp.concatenate([rows8, jnp.zeros_like(rows8)], axis=0).astype(_BF16)


def _row_masks():
    sub = lax.broadcasted_iota(jnp.int32, (8, 2 * N_PAIRS), 0)
    return [((sub & 3) == n).astype(_F32) for n in range(GATHER_NT)]


def _parity_mask():
    lane = lax.broadcasted_iota(jnp.int32, (8, 2 * N_PAIRS), 1)
    sub = lax.broadcasted_iota(jnp.int32, (8, 2 * N_PAIRS), 0)
    return ((sub >> 2) == (lane & 1)).astype(_F32)


def _pipelined_tokens(idx_ref, tbl_ref, tiles, accumulate, drain):
    tb = idx_ref.shape[0]
    n_stage = tb // GATHER_NT
    sets = (tiles[:GATHER_NT], tiles[GATHER_NT:])
    n_acc = len(MXU_ACC)
    slab_tokens = lambda j: [n * n_stage + j for n in range(GATHER_NT)]

    def stage(j, cur, nxt, acc, acc_old):
        tokens = slab_tokens(j)
        drain(slab_tokens(jnp.maximum(j - 2, 0)), acc_old)
        steps = []
        for n in range(GATHER_NT):
            steps += accumulate(tokens[n], n, cur[n], acc, n % 2)
        ahead = slab_tokens(jnp.minimum(j + 1, n_stage - 1))
        part = N_PAIRS // len(steps)
        for i, step in enumerate(steps):
            step()
            _gather_rows(idx_ref, ahead, tbl_ref, nxt, part * i, part * (i + 1))

    n_variant = 2 * n_acc

    def body(j, q):
        for v in range(n_variant):
            @pl.when(q == v)
            def _():
                stage(j, sets[v % 2], sets[1 - v % 2], MXU_ACC[v % n_acc], MXU_ACC[(v + 1) % n_acc])

        return jnp.where(q == n_variant - 1, 0, q + 1)

    @pl.when(pl.program_id(0) == 0)
    def _():
        for acc in MXU_ACC:
            _mxu_pop(acc)

    _gather_rows(idx_ref, slab_tokens(0), tbl_ref, sets[0])
    lax.fori_loop(0, n_stage, body, jnp.int32(0))
    for j in (n_stage - 2, n_stage - 1):
        drain(slab_tokens(j), MXU_ACC[j % n_acc])


def _u_pass_kernel(idx_ref, x_ref, gate_ref, tbl_ref, out_ref, *scratch):
    tiles, ybuf = scratch[:-1], scratch[-1]
    row_masks = _row_masks()
    parity = _parity_mask()
    upper = lax.broadcasted_iota(jnp.int32, (8, HALF_D), 0) >= GATHER_NT

    def accumulate(t, n, tile, acc, reg):
        xrow = x_ref[pl.ds(t, 1), :]
        x8 = jnp.where(upper, jnp.broadcast_to(xrow[:, HALF_D:], (8, HALF_D)),
                       jnp.broadcast_to(xrow[:, :HALF_D], (8, HALF_D)))
        lhs = _slab_lhs(x8 * row_masks[n][:, :1])
        return _mxu_accumulate(acc, [lhs[:, :256], lhs[:, 256:]], tile, reg, transpose=True)

    def drain(tokens, acc):
        z0, z1 = _mxu_pop(acc)
        z = (z0[:8] + z1[:8]) * parity
        y4 = z[:GATHER_NT] + z[GATHER_NT:]
        for n, t in enumerate(tokens):
            ybuf[pl.ds(t, 1), :] = y4[n:n + 1, :]

    _pipelined_tokens(idx_ref, tbl_ref, tiles, accumulate, drain)
    y = ybuf[...]
    lane = lax.broadcasted_iota(jnp.int32, y.shape, 1)
    other = jnp.where((lane & 1) == 0, pltpu.roll(y, 2 * N_PAIRS - 1, axis=1), pltpu.roll(y, 1, axis=1))
    out_ref[...] = gate_ref[...] * _gelu(y + other)


def _v_pass_kernel(idx_ref, w_ref, tbl_ref, out_ref, *tiles):
    row_masks = _row_masks()
    parity = _parity_mask()

    def accumulate(t, n, tile, acc, reg):
        wrow = jnp.broadcast_to(w_ref[pl.ds(t, 1), :], (8, 2 * N_PAIRS))
        lhs = _slab_lhs(wrow * (row_masks[n] * parity))
        return _mxu_accumulate(acc, [lhs, lhs], tile, reg, transpose=False)

    def drain(tokens, acc):
        for h, r in enumerate(_mxu_pop(acc)):
            for b in range(2):
                for n, t in enumerate(tokens):
                    row = GATHER_NT * b + n
                    out_ref[pl.ds(t, 1), pl.ds(HALF_D * b + 256 * h, 256)] = r[row:row + 1, :]

    _pipelined_tokens(idx_ref, tbl_ref, tiles, accumulate, drain)


def _gather_call(kernel_fn, name, n_tok, in_specs, out_spec, out_shape, n_extra_scratch, args):
    tb = GATHER_TB
    tile = pltpu.VMEM((ROW_WORDS * N_PAIRS, 128), jnp.int32)
    return pl.pallas_call(
        kernel_fn,
        out_shape=out_shape,
        grid=(n_tok // tb,),
        in_specs=in_specs,
        out_specs=out_spec,
        scratch_shapes=[tile] * (2 * GATHER_NT) + n_extra_scratch,
        compiler_params=pltpu.CompilerParams(
            dimension_semantics=("arbitrary",),
            vmem_limit_bytes=GATHER_VMEM_BYTES,
        ),
        name=name,
    )(*args)


def _u_pass(idx, h2, gate2, tbl):
    n_tok = idx.shape[0]
    tb = GATHER_TB
    return _gather_call(
        _u_pass_kernel, "peer_u_pass", n_tok,
        [pl.BlockSpec((tb, N_PAIRS), lambda i: (i, 0), memory_space=pltpu.SMEM),
         pl.BlockSpec((tb, D_MODEL), lambda i: (i, 0)),
         pl.BlockSpec((tb, 2 * N_PAIRS), lambda i: (i, 0)),
         _resident()],
        pl.BlockSpec((tb, 2 * N_PAIRS), lambda i: (i, 0)),
        jax.ShapeDtypeStruct((n_tok, 2 * N_PAIRS), _F32),
        [pltpu.VMEM((tb, 2 * N_PAIRS), _F32)],
        (idx, h2, gate2, tbl))


def _v_pass(idx, w2, tbl):
    n_tok = idx.shape[0]
    tb = GATHER_TB
    return _gather_call(
        _v_pass_kernel, "peer_v_pass", n_tok,
        [pl.BlockSpec((tb, N_PAIRS), lambda i: (i, 0), memory_space=pltpu.SMEM),
         pl.BlockSpec((tb, 2 * N_PAIRS), lambda i: (i, 0)),
         _resident()],
        pl.BlockSpec((tb, D_MODEL), lambda i: (i, 0)),
        jax.ShapeDtypeStruct((n_tok, D_MODEL), _F32),
        [],
        (idx, w2, tbl))


def _final_kernel(x1_ref, ffn_ref, mod_ref, g_ref, b_ref, o_ref):
    g2 = mod_ref[0][5:6]
    o_ref[0] = _ln(DEEPNORM_ALPHA * x1_ref[0] + g2 * ffn_ref[0]) * g_ref[...] + b_ref[...]


def _final_norm(x1, ffn, mod3, g, b):
    bsz, seq, d = x1.shape
    ts = NORM_TS
    blk = pl.BlockSpec((1, ts, d), lambda i, s: (i, s, 0))
    vec = pl.BlockSpec((1, d), lambda i, s: (0, 0))
    return pl.pallas_call(
        _final_kernel,
        out_shape=jax.ShapeDtypeStruct((bsz, seq, d), _F32),
        grid=(bsz, seq // ts),
        in_specs=[blk, blk, pl.BlockSpec((1, 6, d), lambda i, s: (i, 0, 0)), vec, vec],
        out_specs=blk,
        compiler_params=pltpu.CompilerParams(dimension_semantics=("arbitrary", "arbitrary")),
        name="final_norm",
    )(x1, ffn, mod3, g, b)


def kernel(x, c, w_cond, b_cond, w_in, gmlp_ln_g, gmlp_ln_b, w_spatial, b_spatial, conv_w, p_a, p_b, w_o, ln1_g, ln1_b, w_q_peer, sub_keys, expert_u, expert_v, ln2_g, ln2_b):
    bsz, seq, d = x.shape
    n_tok = bsz * seq
    depth = w_cond.shape[0]
    row = lambda v: v.reshape(1, d)
    for l in range(depth):
        mod3 = _cond_proj(c, w_cond[l], b_cond[l]).reshape(bsz, 6, d)
        x1, h2 = _mixer(
            x, mod3, w_in[l].astype(_BF16), row(gmlp_ln_g[l]), row(gmlp_ln_b[l]),
            w_spatial[l].astype(_BF16), b_spatial[l][:, :, None], conv_w[l],
            p_a[l].astype(_BF16), p_b[l].astype(_BF16), w_o[l].astype(_BF16),
            row(ln1_g[l]), row(ln1_b[l]))
        h2 = h2.reshape(n_tok, d)
        keys = sub_keys[l].astype(_BF16).reshape(2 * PEER_HEADS, PEER_N_KEYS, PEER_HALF)
        idx, gate2 = _route(h2, w_q_peer[l].astype(_BF16), keys)
        w2 = _u_pass(idx, h2, gate2, _pack_table(expert_u[l]))
        ffn = _v_pass(idx, w2, _pack_table(expert_v[l])).reshape(bsz, seq, d)
        x = _final_norm(x1, ffn, mod3, row(ln2_g[l]), row(ln2_b[l]))
    return x
```

```python
import functools

import jax
import jax.numpy as jnp
from jax import lax
from jax.experimental import pallas as pl
from jax.experimental.pallas import tpu as pltpu

D_MODEL = 1024
CHUNK = 64
GMLP_BLOCK = 128
GMLP_GROUPS = 8
CONV_K = 3
PEER_HEADS = 8
PEER_HALF = 128
PEER_N_KEYS = 128
PEER_TOPK = 16
N_PAIRS = PEER_HEADS * PEER_TOPK
DEEPNORM_ALPHA = 2.0 ** 0.25
LN_EPS = 1e-5

ROW_WORDS = 4
HALF_D = D_MODEL // 2
MIX_TS = 512
ROUTE_TT = 256
GATHER_TB = 256
GATHER_NT = 8
SLAB_ROWS = 2 * GATHER_NT
NORM_TS = 512
PACK_TE = 512
MXU_ACC = (0, 4, 8)
GATHER_VMEM_BYTES = 48 * 1024 * 1024
MIXER_VMEM_BYTES = 56 * 1024 * 1024

_F32 = jnp.float32
_BF16 = jnp.bfloat16
_NEG_INF = float("-inf")


def _ln(x):
    mu = jnp.mean(x, axis=-1, keepdims=True)
    xc = x - mu
    var = jnp.mean(xc * xc, axis=-1, keepdims=True)
    return xc * lax.rsqrt(var + LN_EPS)


def _gelu(x):
    return 0.5 * x * (1.0 + jnp.tanh(0.7978845608028654 * (x + 0.044715 * (x * x * x))))


def _sigmoid(x):
    return 1.0 / (1.0 + jnp.exp(-x))


def _resident():
    return pl.BlockSpec(memory_space=pltpu.VMEM)


def _cond_kernel(c_ref, w_ref, b_ref, o_ref):
    c = c_ref[...]
    a = (c * _sigmoid(c)).astype(_BF16)
    o_ref[...] = jnp.dot(a, w_ref[...].astype(_BF16), preferred_element_type=_F32) + b_ref[...]


def _cond_proj(c, w, b):
    bsz, d = c.shape
    n = w.shape[1]
    tn = 1024
    return pl.pallas_call(
        _cond_kernel,
        out_shape=jax.ShapeDtypeStruct((bsz, n), _F32),
        grid=(n // tn,),
        in_specs=[
            pl.BlockSpec((bsz, d), lambda j: (0, 0)),
            pl.BlockSpec((d, tn), lambda j: (0, j)),
            pl.BlockSpec((1, tn), lambda j: (0, j)),
        ],
        out_specs=pl.BlockSpec((bsz, tn), lambda j: (0, j)),
        compiler_params=pltpu.CompilerParams(dimension_semantics=("arbitrary",)),
        name="cond_proj",
    )(c, w, b.reshape(1, n))


def _mixer_kernel(x_ref, mod_ref, w_in_ref, gg_ref, gb_ref, ws_ref, bs_ref, cw_ref, pa_ref, pb_ref,
                  wo_ref, l1g_ref, l1b_ref, x1_ref, h2_ref, prev_ref, gu_ref, vn_ref, ya_ref):
    ts = x_ref.shape[1]
    d = D_MODEL

    @pl.when(pl.program_id(1) == 0)
    def _():
        prev_ref[...] = jnp.zeros_like(prev_ref)

    x = x_ref[0]
    mod = mod_ref[0]
    sh1, sc1, g1, sh2, sc2 = (mod[i:i + 1] for i in range(5))
    h = (_ln(x) * (1.0 + sc1) + sh1).astype(_BF16)

    def proj(j):
        return jnp.dot(h, w_in_ref[:, j * d:(j + 1) * d], preferred_element_type=_F32)

    gu_ref[...] = _gelu(proj(0))
    vn_ref[...] = (_ln(_gelu(proj(1))) * gg_ref[...] + gb_ref[...]).astype(_BF16)
    qi = lax.broadcasted_iota(jnp.int32, (GMLP_BLOCK, GMLP_BLOCK), 0) // CHUNK
    kj = lax.broadcasted_iota(jnp.int32, (GMLP_BLOCK, GMLP_BLOCK), 1) // CHUNK
    causal = kj <= qi
    for g in range(GMLP_GROUPS):
        wg = jnp.where(causal, ws_ref[g], jnp.zeros((), _BF16))
        cols = slice(g * GMLP_BLOCK, (g + 1) * GMLP_BLOCK)
        for n in range(ts // GMLP_BLOCK):
            rows = slice(n * GMLP_BLOCK, (n + 1) * GMLP_BLOCK)
            mixed = jnp.dot(wg, vn_ref[rows, cols], preferred_element_type=_F32) + bs_ref[g]
            ya_ref[rows, cols] = (gu_ref[rows, cols] * mixed).astype(_BF16)

    g_b = proj(2)
    zc = proj(3) * proj(4)
    ext = jnp.concatenate([prev_ref[...], zc], axis=0)
    prev_ref[...] = zc[ts - 8:, :]
    cw = cw_ref[...]
    y = ext[6:ts + 6, :] * cw[0:1] + ext[7:ts + 7, :] * cw[1:2] + zc * cw[2:3]
    y_b = (g_b * y).astype(_BF16)

    merged = (_sigmoid(proj(5)) * jnp.dot(ya_ref[...], pa_ref[...], preferred_element_type=_F32)
              + _sigmoid(proj(6)) * jnp.dot(y_b, pb_ref[...], preferred_element_type=_F32))
    mix = jnp.dot(merged.astype(_BF16), wo_ref[...], preferred_element_type=_F32)
    x1 = _ln(DEEPNORM_ALPHA * x + g1 * mix) * l1g_ref[...] + l1b_ref[...]
    x1_ref[0] = x1
    h2_ref[0] = _ln(x1) * (1.0 + sc2) + sh2


def _mixer(x, mod3, w_in, gg, gb, ws, bs, cw, pa, pb, wo, l1g, l1b):
    bsz, seq, d = x.shape
    ts = MIX_TS
    blk = pl.BlockSpec((1, ts, d), lambda b, s: (b, s, 0))
    return pl.pallas_call(
        _mixer_kernel,
        out_shape=(jax.ShapeDtypeStruct((bsz, seq, d), _F32),
                   jax.ShapeDtypeStruct((bsz, seq, d), _F32)),
        grid=(bsz, seq // ts),
        in_specs=[blk, pl.BlockSpec((1, 6, d), lambda b, s: (b, 0, 0))] + [_resident()] * 11,
        out_specs=(blk, blk),
        scratch_shapes=[
            pltpu.VMEM((8, d), _F32),
            pltpu.VMEM((ts, d), _F32),
            pltpu.VMEM((ts, d), _BF16),
            pltpu.VMEM((ts, d), _BF16),
        ],
        compiler_params=pltpu.CompilerParams(
            dimension_semantics=("arbitrary", "arbitrary"),
            vmem_limit_bytes=MIXER_VMEM_BYTES,
        ),
        name="mixer",
    )(x, mod3, w_in, gg, gb, ws, bs, cw, pa, pb, wo, l1g, l1b)


def _sorted_top(vals, ids, payload, k):
    vals, ids = list(vals), list(ids)
    pay = None if payload is None else list(payload)
    g = len(vals)
    for phase in range(g):
        for i in range(phase % 2, g - 1, 2):
            swap = vals[i + 1] > vals[i]
            vals[i], vals[i + 1] = jnp.maximum(vals[i], vals[i + 1]), jnp.minimum(vals[i], vals[i + 1])
            ids[i], ids[i + 1] = jnp.where(swap, ids[i + 1], ids[i]), jnp.where(swap, ids[i], ids[i + 1])
            if pay is not None:
                pay[i], pay[i + 1] = jnp.where(swap, pay[i + 1], pay[i]), jnp.where(swap, pay[i], pay[i + 1])
    big = jnp.int32(2 ** 30)
    top_v, top_p = [], []
    for r in range(k):
        m = jnp.max(vals[0], axis=0, keepdims=True)
        sel = jnp.min(jnp.where(vals[0] == m, ids[0], big), axis=0, keepdims=True)
        hit = ids[0] == sel
        top_v.append(m)
        top_p.append(sel if pay is None else jnp.max(jnp.where(hit, pay[0], -1), axis=0, keepdims=True))
        for i in range(min(g, k - r - 1)):
            nxt = i + 1 < g
            vals[i] = jnp.where(hit, vals[i + 1] if nxt else _NEG_INF, vals[i])
            ids[i] = jnp.where(hit, ids[i + 1] if nxt else big, ids[i])
            if pay is not None and nxt:
                pay[i] = jnp.where(hit, pay[i + 1], pay[i])
    return top_v, top_p


def _route_kernel(h_ref, wq_ref, keys_ref, idx_ref, gate_ref, st_ref, it_ref, idx_t_ref, gate_t_ref):
    tt = h_ref.shape[0]
    lanes = 128
    q = jnp.dot(h_ref[...].astype(_BF16), wq_ref[...], preferred_element_type=_F32).astype(_BF16)
    row = lax.broadcasted_iota(jnp.int32, (8, lanes), 0)
    groups = PEER_N_KEYS // 8
    for hp in range(2 * PEER_HEADS):
        s = lax.dot_general(keys_ref[hp], q[:, hp * PEER_HALF:(hp + 1) * PEER_HALF],
                            (((1,), (1,)), ((), ())), preferred_element_type=_F32)
        for c in range(tt // lanes):
            cols = slice(c * lanes, (c + 1) * lanes)
            top_v, top_i = _sorted_top([s[8 * i:8 * i + 8, cols] for i in range(groups)],
                                       [row + 8 * i for i in range(groups)], None, PEER_TOPK)
            for r in range(PEER_TOPK):
                st_ref[hp, r:r + 1, cols] = top_v[r]
                it_ref[hp, r:r + 1, cols] = top_i[r]

    for hd in range(PEER_HEADS):
        for c in range(tt // lanes):
            cols = slice(c * lanes, (c + 1) * lanes)
            s1, s2 = st_ref[2 * hd, :, cols], st_ref[2 * hd + 1, :, cols]
            i1, i2 = it_ref[2 * hd, :, cols], it_ref[2 * hd + 1, :, cols]
            vals, eids, cids = [], [], []
            for a in range(8):
                vals.append(s1[a:a + 1] + s2[0:8])
                eids.append(i1[a:a + 1] * PEER_N_KEYS + i2[0:8])
                cids.append(a * PEER_TOPK + row)
                if a == 0:
                    vals.append(s1[0:1] + s2[8:16])
                    eids.append(i1[0:1] * PEER_N_KEYS + i2[8:16])
                    cids.append(8 + row)
            vals.append(s1[8:16] + s2[0:1])
            eids.append(i1[8:16] * PEER_N_KEYS + i2[0:1])
            cids.append((8 + row) * PEER_TOPK)
            top_s, top_e = _sorted_top(vals, cids, eids, PEER_TOPK)
            ex = [jnp.exp(v - top_s[0]) for v in top_s]
            denom = functools.reduce(lambda p, r: p + r, ex)
            for r in range(PEER_TOPK):
                k = hd * PEER_TOPK + r
                idx_t_ref[k:k + 1, cols] = top_e[r] * ROW_WORDS
                gate_t_ref[2 * k:2 * k + 2, cols] = jnp.broadcast_to(ex[r] / denom, (2, lanes))
    idx_ref[...] = idx_t_ref[...].T
    gate_ref[...] = gate_t_ref[...].T


def _route(h2, wq, keys):
    n_tok, d = h2.shape
    tt = ROUTE_TT
    return pl.pallas_call(
        _route_kernel,
        out_shape=(jax.ShapeDtypeStruct((n_tok, N_PAIRS), jnp.int32),
                   jax.ShapeDtypeStruct((n_tok, 2 * N_PAIRS), _F32)),
        grid=(n_tok // tt,),
        in_specs=[pl.BlockSpec((tt, d), lambda i: (i, 0)), _resident(), _resident()],
        out_specs=(pl.BlockSpec((tt, N_PAIRS), lambda i: (i, 0)),
                   pl.BlockSpec((tt, 2 * N_PAIRS), lambda i: (i, 0))),
        scratch_shapes=[
            pltpu.VMEM((2 * PEER_HEADS, PEER_TOPK, tt), _F32),
            pltpu.VMEM((2 * PEER_HEADS, PEER_TOPK, tt), jnp.int32),
            pltpu.VMEM((N_PAIRS, tt), jnp.int32),
            pltpu.VMEM((2 * N_PAIRS, tt), _F32),
        ],
        compiler_params=pltpu.CompilerParams(dimension_semantics=("arbitrary",)),
        name="peer_route",
    )(h2, wq, keys)


def _pack_kernel(w_ref, o_ref):
    n = w_ref.shape[0]
    w = w_ref[...]
    lo = pltpu.bitcast(w[:, :HALF_D].astype(_BF16).astype(_F32), jnp.uint32) >> 16
    hi = pltpu.bitcast(w[:, HALF_D:].astype(_BF16).astype(_F32), jnp.uint32) & jnp.uint32(0xFFFF0000)
    words = pltpu.bitcast(lo | hi, jnp.int32)
    for r in range(ROW_WORDS):
        o_ref[pl.ds(r, n, stride=ROW_WORDS), :] = words[:, 128 * r:128 * (r + 1)]


def _pack_table(w):
    n, d = w.shape
    te = PACK_TE
    return pl.pallas_call(
        _pack_kernel,
        out_shape=jax.ShapeDtypeStruct((n * ROW_WORDS, 128), jnp.int32),
        grid=(n // te,),
        in_specs=[pl.BlockSpec((te, d), lambda i: (i, 0))],
        out_specs=pl.BlockSpec((te * ROW_WORDS, 128), lambda i: (i, 0)),
        compiler_params=pltpu.CompilerParams(dimension_semantics=("arbitrary",)),
        name="pack_table",
    )(w)


def _gather_rows(idx_ref, tokens, tbl_ref, slab_tiles, lo=0, hi=N_PAIRS):
    rows = [idx_ref.at[t] for t in tokens]
    for k in range(lo, hi):
        for row, tile in zip(rows, slab_tiles):
            i = pl.multiple_of(row[k], ROW_WORDS)
            tile[pl.ds(ROW_WORDS * k, ROW_WORDS), :] = tbl_ref[pl.ds(i, ROW_WORDS), :]


def _rhs_half(tile, h):
    parts = [pltpu.bitcast(tile[pl.ds(2 * h + r, N_PAIRS, stride=ROW_WORDS), :], _BF16)
             for r in range(2)]
    return jnp.concatenate(parts, axis=1)


def _mxu_accumulate(acc, lhs_halves, tile, reg, transpose):
    def step(h):
        pltpu.matmul_push_rhs(_rhs_half(tile, h), staging_register=reg, mxu_index=h, transpose=transpose)
        pltpu.matmul_acc_lhs(acc, lhs_halves[h], mxu_index=h, load_staged_rhs=reg)

    return [functools.partial(step, h) for h in range(2)]


def _mxu_pop(acc):
    return [pltpu.matmul_pop(acc, (SLAB_ROWS, 256), _F32, mxu_index=h) for h in range(2)]


def _row_masks():
    sub = lax.broadcasted_iota(jnp.int32, (SLAB_ROWS, 2 * N_PAIRS), 0)
    return [((sub % GATHER_NT) == n).astype(_F32) for n in range(GATHER_NT)]


def _parity_mask():
    lane = lax.broadcasted_iota(jnp.int32, (SLAB_ROWS, 2 * N_PAIRS), 1)
    sub = lax.broadcasted_iota(jnp.int32, (SLAB_ROWS, 2 * N_PAIRS), 0)
    return ((sub // GATHER_NT) == (lane & 1)).astype(_F32)


def _pipelined_tokens(idx_ref, tbl_ref, tiles, accumulate, drain):
    tb = idx_ref.shape[0]
    n_stage = tb // GATHER_NT
    sets = (tiles[:GATHER_NT], tiles[GATHER_NT:])
    n_acc = len(MXU_ACC)
    slab_tokens = lambda j: [n * n_stage + j for n in range(GATHER_NT)]

    def stage(j, cur, nxt, acc, acc_old):
        tokens = slab_tokens(j)
        drain(slab_tokens(jnp.maximum(j - 2, 0)), acc_old)
        steps = []
        for n in range(GATHER_NT):
            steps += accumulate(tokens[n], n, cur[n], acc, n % 2)
        ahead = slab_tokens(jnp.minimum(j + 1, n_stage - 1))
        part = N_PAIRS // len(steps)
        for i, step in enumerate(steps):
            step()
            _gather_rows(idx_ref, ahead, tbl_ref, nxt, part * i, part * (i + 1))

    n_variant = 2 * n_acc

    def body(j, q):
        for v in range(n_variant):
            @pl.when(q == v)
            def _():
                stage(j, sets[v % 2], sets[1 - v % 2], MXU_ACC[v % n_acc], MXU_ACC[(v + 1) % n_acc])

        return jnp.where(q == n_variant - 1, 0, q + 1)

    @pl.when(pl.program_id(0) == 0)
    def _():
        for acc in MXU_ACC:
            _mxu_pop(acc)

    _gather_rows(idx_ref, slab_tokens(0), tbl_ref, sets[0])
    lax.fori_loop(0, n_stage, body, jnp.int32(0))
    for j in (n_stage - 2, n_stage - 1):
        drain(slab_tokens(j), MXU_ACC[j % n_acc])


def _u_pass_kernel(idx_ref, x_ref, gate_ref, tbl_ref, out_ref, *scratch):
    tiles, ybuf = scratch[:-1], scratch[-1]
    row_masks = _row_masks()
    parity = _parity_mask()
    upper = lax.broadcasted_iota(jnp.int32, (SLAB_ROWS, HALF_D), 0) >= GATHER_NT

    def accumulate(t, n, tile, acc, reg):
        xrow = x_ref[pl.ds(t, 1), :]
        xs = jnp.where(upper, jnp.broadcast_to(xrow[:, HALF_D:], (SLAB_ROWS, HALF_D)),
                       jnp.broadcast_to(xrow[:, :HALF_D], (SLAB_ROWS, HALF_D)))
        lhs = (xs * row_masks[n][:, :1]).astype(_BF16)
        return _mxu_accumulate(acc, [lhs[:, :256], lhs[:, 256:]], tile, reg, transpose=True)

    def drain(tokens, acc):
        z0, z1 = _mxu_pop(acc)
        z = (z0 + z1) * parity
        ys = z[:GATHER_NT] + z[GATHER_NT:]
        for n, t in enumerate(tokens):
            ybuf[pl.ds(t, 1), :] = ys[n:n + 1, :]

    _pipelined_tokens(idx_ref, tbl_ref, tiles, accumulate, drain)
    y = ybuf[...]
    lane = lax.broadcasted_iota(jnp.int32, y.shape, 1)
    other = jnp.where((lane & 1) == 0, pltpu.roll(y, 2 * N_PAIRS - 1, axis=1), pltpu.roll(y, 1, axis=1))
    out_ref[...] = gate_ref[...] * _gelu(y + other)


def _v_pass_kernel(idx_ref, w_ref, tbl_ref, out_ref, *tiles):
    row_masks = _row_masks()
    parity = _parity_mask()

    def accumulate(t, n, tile, acc, reg):
        wrow = jnp.broadcast_to(w_ref[pl.ds(t, 1), :], (SLAB_ROWS, 2 * N_PAIRS))
        lhs = (wrow * (row_masks[n] * parity)).astype(_BF16)
        return _mxu_accumulate(acc, [lhs, lhs], tile, reg, transpose=False)

    def drain(tokens, acc):
        for h, r in enumerate(_mxu_pop(acc)):
            for b in range(2):
                for n, t in enumerate(tokens):
                    row = GATHER_NT * b + n
                    out_ref[pl.ds(t, 1), pl.ds(HALF_D * b + 256 * h, 256)] = r[row:row + 1, :]

    _pipelined_tokens(idx_ref, tbl_ref, tiles, accumulate, drain)


def _gather_call(kernel_fn, name, n_tok, in_specs, out_spec, out_shape, n_extra_scratch, args):
    tb = GATHER_TB
    tile = pltpu.VMEM((ROW_WORDS * N_PAIRS, 128), jnp.int32)
    return pl.pallas_call(
        kernel_fn,
        out_shape=out_shape,
        grid=(n_tok // tb,),
        in_specs=in_specs,
        out_specs=out_spec,
        scratch_shapes=[tile] * (2 * GATHER_NT) + n_extra_scratch,
        compiler_params=pltpu.CompilerParams(
            dimension_semantics=("arbitrary",),
            vmem_limit_bytes=GATHER_VMEM_BYTES,
        ),
        name=name,
    )(*args)


def _u_pass(idx, h2, gate2, tbl):
    n_tok = idx.shape[0]
    tb = GATHER_TB
    return _gather_call(
        _u_pass_kernel, "peer_u_pass", n_tok,
        [pl.BlockSpec((tb, N_PAIRS), lambda i: (i, 0), memory_space=pltpu.SMEM),
         pl.BlockSpec((tb, D_MODEL), lambda i: (i, 0)),
         pl.BlockSpec((tb, 2 * N_PAIRS), lambda i: (i, 0)),
         _resident()],
        pl.BlockSpec((tb, 2 * N_PAIRS), lambda i: (i, 0)),
        jax.ShapeDtypeStruct((n_tok, 2 * N_PAIRS), _F32),
        [pltpu.VMEM((tb, 2 * N_PAIRS), _F32)],
        (idx, h2, gate2, tbl))


def _v_pass(idx, w2, tbl):
    n_tok = idx.shape[0]
    tb = GATHER_TB
    return _gather_call(
        _v_pass_kernel, "peer_v_pass", n_tok,
        [pl.BlockSpec((tb, N_PAIRS), lambda i: (i, 0), memory_space=pltpu.SMEM),
         pl.BlockSpec((tb, 2 * N_PAIRS), lambda i: (i, 0)),
         _resident()],
        pl.BlockSpec((tb, D_MODEL), lambda i: (i, 0)),
        jax.ShapeDtypeStruct((n_tok, D_MODEL), _F32),
        [],
        (idx, w2, tbl))


def _final_kernel(x1_ref, ffn_ref, mod_ref, g_ref, b_ref, o_ref):
    g2 = mod_ref[0][5:6]
    o_ref[0] = _ln(DEEPNORM_ALPHA * x1_ref[0] + g2 * ffn_ref[0]) * g_ref[...] + b_ref[...]


def _final_norm(x1, ffn, mod3, g, b):
    bsz, seq, d = x1.shape
    ts = NORM_TS
    blk = pl.BlockSpec((1, ts, d), lambda i, s: (i, s, 0))
    vec = pl.BlockSpec((1, d), lambda i, s: (0, 0))
    return pl.pallas_call(
        _final_kernel,
        out_shape=jax.ShapeDtypeStruct((bsz, seq, d), _F32),
        grid=(bsz, seq // ts),
        in_specs=[blk, blk, pl.BlockSpec((1, 6, d), lambda i, s: (i, 0, 0)), vec, vec],
        out_specs=blk,
        compiler_params=pltpu.CompilerParams(dimension_semantics=("arbitrary", "arbitrary")),
        name="final_norm",
    )(x1, ffn, mod3, g, b)


def kernel(x, c, w_cond, b_cond, w_in, gmlp_ln_g, gmlp_ln_b, w_spatial, b_spatial, conv_w, p_a, p_b, w_o, ln1_g, ln1_b, w_q_peer, sub_keys, expert_u, expert_v, ln2_g, ln2_b):
    bsz, seq, d = x.shape
    n_tok = bsz * seq
    depth = w_cond.shape[0]
    row = lambda v: v.reshape(1, d)
    for l in range(depth):
        mod3 = _cond_proj(c, w_cond[l], b_cond[l]).reshape(bsz, 6, d)
        x1, h2 = _mixer(
            x, mod3, w_in[l].astype(_BF16), row(gmlp_ln_g[l]), row(gmlp_ln_b[l]),
            w_spatial[l].astype(_BF16), b_spatial[l][:, :, None], conv_w[l],
            p_a[l].astype(_BF16), p_b[l].astype(_BF16), w_o[l].astype(_BF16),
            row(ln1_g[l]), row(ln1_b[l]))
        h2 = h2.reshape(n_tok, d)
        keys = sub_keys[l].astype(_BF16).reshape(2 * PEER_HEADS, PEER_N_KEYS, PEER_HALF)
        idx, gate2 = _route(h2, w_q_peer[l].astype(_BF16), keys)
        w2 = _u_pass(idx, h2, gate2, _pack_table(expert_u[l]))
        ffn = _v_pass(idx, w2, _pack_table(expert_v[l])).reshape(bsz, seq, d)
        x = _final_norm(x1, ffn, mod3, row(ln2_g[l]), row(ln2_b[l]))
    return x
```

```python
import functools

import jax
import jax.numpy as jnp
from jax import lax
from jax.experimental import pallas as pl
from jax.experimental.pallas import tpu as pltpu

D_MODEL = 1024
CHUNK = 64
GMLP_BLOCK = 128
GMLP_GROUPS = 8
CONV_K = 3
PEER_HEADS = 8
PEER_HALF = 128
PEER_N_KEYS = 128
PEER_TOPK = 16
N_PAIRS = PEER_HEADS * PEER_TOPK
DEEPNORM_ALPHA = 2.0 ** 0.25
LN_EPS = 1e-5

ROW_WORDS = 4
HALF_D = D_MODEL // 2
MIX_TS = 512
ROUTE_TT = 256
GATHER_TB = 256
GATHER_NT = 8
SLAB_ROWS = 2 * GATHER_NT
NORM_TS = 512
PACK_TE = 512
MXU_ACC = (0, 4)
GATHER_VMEM_BYTES = 48 * 1024 * 1024
MIXER_VMEM_BYTES = 56 * 1024 * 1024

_F32 = jnp.float32
_BF16 = jnp.bfloat16
_NEG_INF = float("-inf")


def _ln(x):
    mu = jnp.mean(x, axis=-1, keepdims=True)
    xc = x - mu
    var = jnp.mean(xc * xc, axis=-1, keepdims=True)
    return xc * lax.rsqrt(var + LN_EPS)


def _gelu(x):
    return 0.5 * x * (1.0 + jnp.tanh(0.7978845608028654 * (x + 0.044715 * (x * x * x))))


def _sigmoid(x):
    return 1.0 / (1.0 + jnp.exp(-x))


def _resident():
    return pl.BlockSpec(memory_space=pltpu.VMEM)


def _cond_kernel(c_ref, w_ref, b_ref, o_ref):
    c = c_ref[...]
    a = (c * _sigmoid(c)).astype(_BF16)
    o_ref[...] = jnp.dot(a, w_ref[...].astype(_BF16), preferred_element_type=_F32) + b_ref[...]


def _cond_proj(c, w, b):
    bsz, d = c.shape
    n = w.shape[1]
    tn = 1024
    return pl.pallas_call(
        _cond_kernel,
        out_shape=jax.ShapeDtypeStruct((bsz, n), _F32),
        grid=(n // tn,),
        in_specs=[
            pl.BlockSpec((bsz, d), lambda j: (0, 0)),
            pl.BlockSpec((d, tn), lambda j: (0, j)),
            pl.BlockSpec((1, tn), lambda j: (0, j)),
        ],
        out_specs=pl.BlockSpec((bsz, tn), lambda j: (0, j)),
        compiler_params=pltpu.CompilerParams(dimension_semantics=("arbitrary",)),
        name="cond_proj",
    )(c, w, b.reshape(1, n))


def _mixer_kernel(x_ref, mod_ref, w_in_ref, gg_ref, gb_ref, ws_ref, bs_ref, cw_ref, pa_ref, pb_ref,
                  wo_ref, l1g_ref, l1b_ref, x1_ref, h2_ref, prev_ref, gu_ref, vn_ref, ya_ref):
    ts = x_ref.shape[1]
    d = D_MODEL

    @pl.when(pl.program_id(1) == 0)
    def _():
        prev_ref[...] = jnp.zeros_like(prev_ref)

    x = x_ref[0]
    mod = mod_ref[0]
    sh1, sc1, g1, sh2, sc2 = (mod[i:i + 1] for i in range(5))
    h = (_ln(x) * (1.0 + sc1) + sh1).astype(_BF16)

    def proj(j):
        return jnp.dot(h, w_in_ref[:, j * d:(j + 1) * d], preferred_element_type=_F32)

    gu_ref[...] = _gelu(proj(0))
    vn_ref[...] = (_ln(_gelu(proj(1))) * gg_ref[...] + gb_ref[...]).astype(_BF16)
    qi = lax.broadcasted_iota(jnp.int32, (GMLP_BLOCK, GMLP_BLOCK), 0) // CHUNK
    kj = lax.broadcasted_iota(jnp.int32, (GMLP_BLOCK, GMLP_BLOCK), 1) // CHUNK
    causal = kj <= qi
    for g in range(GMLP_GROUPS):
        wg = jnp.where(causal, ws_ref[g], jnp.zeros((), _BF16))
        cols = slice(g * GMLP_BLOCK, (g + 1) * GMLP_BLOCK)
        for n in range(ts // GMLP_BLOCK):
            rows = slice(n * GMLP_BLOCK, (n + 1) * GMLP_BLOCK)
            mixed = jnp.dot(wg, vn_ref[rows, cols], preferred_element_type=_F32) + bs_ref[g]
            ya_ref[rows, cols] = (gu_ref[rows, cols] * mixed).astype(_BF16)

    g_b = proj(2)
    zc = proj(3) * proj(4)
    ext = jnp.concatenate([prev_ref[...], zc], axis=0)
    prev_ref[...] = zc[ts - 8:, :]
    cw = cw_ref[...]
    y = ext[6:ts + 6, :] * cw[0:1] + ext[7:ts + 7, :] * cw[1:2] + zc * cw[2:3]
    y_b = (g_b * y).astype(_BF16)

    merged = (_sigmoid(proj(5)) * jnp.dot(ya_ref[...], pa_ref[...], preferred_element_type=_F32)
              + _sigmoid(proj(6)) * jnp.dot(y_b, pb_ref[...], preferred_element_type=_F32))
    mix = jnp.dot(merged.astype(_BF16), wo_ref[...], preferred_element_type=_F32)
    x1 = _ln(DEEPNORM_ALPHA * x + g1 * mix) * l1g_ref[...] + l1b_ref[...]
    x1_ref[0] = x1
    h2_ref[0] = _ln(x1) * (1.0 + sc2) + sh2


def _mixer(x, mod3, w_in, gg, gb, ws, bs, cw, pa, pb, wo, l1g, l1b):
    bsz, seq, d = x.shape
    ts = MIX_TS
    blk = pl.BlockSpec((1, ts, d), lambda b, s: (b, s, 0))
    return pl.pallas_call(
        _mixer_kernel,
        out_shape=(jax.ShapeDtypeStruct((bsz, seq, d), _F32),
                   jax.ShapeDtypeStruct((bsz, seq, d), _F32)),
        grid=(bsz, seq // ts),
        in_specs=[blk, pl.BlockSpec((1, 6, d), lambda b, s: (b, 0, 0))] + [_resident()] * 11,
        out_specs=(blk, blk),
        scratch_shapes=[
            pltpu.VMEM((8, d), _F32),
            pltpu.VMEM((ts, d), _F32),
            pltpu.VMEM((ts, d), _BF16),
            pltpu.VMEM((ts, d), _BF16),
        ],
        compiler_params=pltpu.CompilerParams(
            dimension_semantics=("arbitrary", "arbitrary"),
            vmem_limit_bytes=MIXER_VMEM_BYTES,
        ),
        name="mixer",
    )(x, mod3, w_in, gg, gb, ws, bs, cw, pa, pb, wo, l1g, l1b)


def _sorted_top(vals, ids, payload, k):
    vals, ids = list(vals), list(ids)
    pay = None if payload is None else list(payload)
    g = len(vals)
    for phase in range(g):
        for i in range(phase % 2, g - 1, 2):
            swap = vals[i + 1] > vals[i]
            vals[i], vals[i + 1] = jnp.maximum(vals[i], vals[i + 1]), jnp.minimum(vals[i], vals[i + 1])
            ids[i], ids[i + 1] = jnp.where(swap, ids[i + 1], ids[i]), jnp.where(swap, ids[i], ids[i + 1])
            if pay is not None:
                pay[i], pay[i + 1] = jnp.where(swap, pay[i + 1], pay[i]), jnp.where(swap, pay[i], pay[i + 1])
    big = jnp.int32(2 ** 30)
    top_v, top_p = [], []
    for r in range(k):
        m = jnp.max(vals[0], axis=0, keepdims=True)
        sel = jnp.min(jnp.where(vals[0] == m, ids[0], big), axis=0, keepdims=True)
        hit = ids[0] == sel
        top_v.append(m)
        top_p.append(sel if pay is None else jnp.max(jnp.where(hit, pay[0], -1), axis=0, keepdims=True))
        for i in range(min(g, k - r - 1)):
            nxt = i + 1 < g
            vals[i] = jnp.where(hit, vals[i + 1] if nxt else _NEG_INF, vals[i])
            ids[i] = jnp.where(hit, ids[i + 1] if nxt else big, ids[i])
            if pay is not None and nxt:
                pay[i] = jnp.where(hit, pay[i + 1], pay[i])
    return top_v, top_p


def _route_kernel(h_ref, wq_ref, keys_ref, idx_ref, gate_ref, st_ref, it_ref, idx_t_ref, gate_t_ref):
    tt = h_ref.shape[0]
    lanes = 128
    q = jnp.dot(h_ref[...].astype(_BF16), wq_ref[...], preferred_element_type=_F32).astype(_BF16)
    row = lax.broadcasted_iota(jnp.int32, (8, lanes), 0)
    groups = PEER_N_KEYS // 8
    for hp in range(2 * PEER_HEADS):
        s = lax.dot_general(keys_ref[hp], q[:, hp * PEER_HALF:(hp + 1) * PEER_HALF],
                            (((1,), (1,)), ((), ())), preferred_element_type=_F32)
        for c in range(tt // lanes):
            cols = slice(c * lanes, (c + 1) * lanes)
            top_v, top_i = _sorted_top([s[8 * i:8 * i + 8, cols] for i in range(groups)],
                                       [row + 8 * i for i in range(groups)], None, PEER_TOPK)
            for r in range(PEER_TOPK):
                st_ref[hp, r:r + 1, cols] = top_v[r]
                it_ref[hp, r:r + 1, cols] = top_i[r]

    for hd in range(PEER_HEADS):
        for c in range(tt // lanes):
            cols = slice(c * lanes, (c + 1) * lanes)
            s1, s2 = st_ref[2 * hd, :, cols], st_ref[2 * hd + 1, :, cols]
            i1, i2 = it_ref[2 * hd, :, cols], it_ref[2 * hd + 1, :, cols]
            vals, eids, cids = [], [], []
            for a in range(8):
                vals.append(s1[a:a + 1] + s2[0:8])
                eids.append(i1[a:a + 1] * PEER_N_KEYS + i2[0:8])
                cids.append(a * PEER_TOPK + row)
                if a == 0:
                    vals.append(s1[0:1] + s2[8:16])
                    eids.append(i1[0:1] * PEER_N_KEYS + i2[8:16])
                    cids.append(8 + row)
            vals.append(s1[8:16] + s2[0:1])
            eids.append(i1[8:16] * PEER_N_KEYS + i2[0:1])
            cids.append((8 + row) * PEER_TOPK)
            top_s, top_e = _sorted_top(vals, cids, eids, PEER_TOPK)
            ex = [jnp.exp(v - top_s[0]) for v in top_s]
            denom = functools.reduce(lambda p, r: p + r, ex)
            for r in range(PEER_TOPK):
                k = hd * PEER_TOPK + r
                idx_t_ref[k:k + 1, cols] = top_e[r] * ROW_WORDS
                gate_t_ref[2 * k:2 * k + 2, cols] = jnp.broadcast_to(ex[r] / denom, (2, lanes))
    idx_ref[...] = idx_t_ref[...].T
    gate_ref[...] = gate_t_ref[...].T


def _route(h2, wq, keys):
    n_tok, d = h2.shape
    tt = ROUTE_TT
    return pl.pallas_call(
        _route_kernel,
        out_shape=(jax.ShapeDtypeStruct((n_tok, N_PAIRS), jnp.int32),
                   jax.ShapeDtypeStruct((n_tok, 2 * N_PAIRS), _F32)),
        grid=(n_tok // tt,),
        in_specs=[pl.BlockSpec((tt, d), lambda i: (i, 0)), _resident(), _resident()],
        out_specs=(pl.BlockSpec((tt, N_PAIRS), lambda i: (i, 0)),
                   pl.BlockSpec((tt, 2 * N_PAIRS), lambda i: (i, 0))),
        scratch_shapes=[
            pltpu.VMEM((2 * PEER_HEADS, PEER_TOPK, tt), _F32),
            pltpu.VMEM((2 * PEER_HEADS, PEER_TOPK, tt), jnp.int32),
            pltpu.VMEM((N_PAIRS, tt), jnp.int32),
            pltpu.VMEM((2 * N_PAIRS, tt), _F32),
        ],
        compiler_params=pltpu.CompilerParams(dimension_semantics=("arbitrary",)),
        name="peer_route",
    )(h2, wq, keys)


def _pack_kernel(w_ref, o_ref):
    n = w_ref.shape[0]
    w = w_ref[...]
    lo = pltpu.bitcast(w[:, :HALF_D].astype(_BF16).astype(_F32), jnp.uint32) >> 16
    hi = pltpu.bitcast(w[:, HALF_D:].astype(_BF16).astype(_F32), jnp.uint32) & jnp.uint32(0xFFFF0000)
    words = pltpu.bitcast(lo | hi, jnp.int32)
    for r in range(ROW_WORDS):
        o_ref[pl.ds(r, n, stride=ROW_WORDS), :] = words[:, 128 * r:128 * (r + 1)]


def _pack_table(w):
    n, d = w.shape
    te = PACK_TE
    return pl.pallas_call(
        _pack_kernel,
        out_shape=jax.ShapeDtypeStruct((n * ROW_WORDS, 128), jnp.int32),
        grid=(n // te,),
        in_specs=[pl.BlockSpec((te, d), lambda i: (i, 0))],
        out_specs=pl.BlockSpec((te * ROW_WORDS, 128), lambda i: (i, 0)),
        compiler_params=pltpu.CompilerParams(dimension_semantics=("arbitrary",)),
        name="pack_table",
    )(w)


def _gather_rows(idx_ref, tokens, tbl_ref, slab_tiles, lo=0, hi=N_PAIRS):
    rows = [idx_ref.at[t] for t in tokens]
    for k in range(lo, hi):
        for row, tile in zip(rows, slab_tiles):
            i = pl.multiple_of(row[k], ROW_WORDS)
            tile[pl.ds(ROW_WORDS * k, ROW_WORDS), :] = tbl_ref[pl.ds(i, ROW_WORDS), :]


def _rhs_half(tile, h):
    parts = [pltpu.bitcast(tile[pl.ds(2 * h + r, N_PAIRS, stride=ROW_WORDS), :], _BF16)
             for r in range(2)]
    return jnp.concatenate(parts, axis=1)


def _mxu_accumulate(acc, lhs_halves, tile, reg, transpose):
    def step(h):
        pltpu.matmul_push_rhs(_rhs_half(tile, h), staging_register=reg, mxu_index=h, transpose=transpose)
        pltpu.matmul_acc_lhs(acc, lhs_halves[h], mxu_index=h, load_staged_rhs=reg)

    return [functools.partial(step, h) for h in range(2)]


def _mxu_pop(acc):
    return [pltpu.matmul_pop(acc, (SLAB_ROWS, 256), _F32, mxu_index=h) for h in range(2)]


def _row_masks():
    sub = lax.broadcasted_iota(jnp.int32, (SLAB_ROWS, 2 * N_PAIRS), 0)
    return [((sub % GATHER_NT) == n).astype(_F32) for n in range(GATHER_NT)]


def _parity_mask():
    lane = lax.broadcasted_iota(jnp.int32, (SLAB_ROWS, 2 * N_PAIRS), 1)
    sub = lax.broadcasted_iota(jnp.int32, (SLAB_ROWS, 2 * N_PAIRS), 0)
    return ((sub // GATHER_NT) == (lane & 1)).astype(_F32)


def _pipelined_tokens(idx_ref, tbl_ref, tiles, accumulate, drain):
    tb = idx_ref.shape[0]
    n_stage = tb // GATHER_NT
    sets = (tiles[:GATHER_NT], tiles[GATHER_NT:])
    slab_tokens = lambda j: [n * n_stage + j for n in range(GATHER_NT)]

    def stage(j, cur, nxt, acc):
        tokens = slab_tokens(j)
        drain(slab_tokens(jnp.maximum(j - 2, 0)), acc)
        steps = []
        for n in range(GATHER_NT):
            steps += accumulate(tokens[n], n, cur[n], acc, n % 2)
        ahead = slab_tokens(jnp.minimum(j + 1, n_stage - 1))
        part = N_PAIRS // len(steps)
        for i, step in enumerate(steps):
            step()
            _gather_rows(idx_ref, ahead, tbl_ref, nxt, part * i, part * (i + 1))

    def body(j, carry):
        for p in range(2):
            @pl.when((j & 1) == p)
            def _():
                stage(j, sets[p], sets[1 - p], MXU_ACC[p])

        return carry

    @pl.when(pl.program_id(0) == 0)
    def _():
        for acc in MXU_ACC:
            _mxu_pop(acc)

    _gather_rows(idx_ref, slab_tokens(0), tbl_ref, sets[0])
    lax.fori_loop(0, n_stage, body, 0)
    for j in (n_stage - 2, n_stage - 1):
        drain(slab_tokens(j), MXU_ACC[j % 2])


def _u_pass_kernel(idx_ref, x_ref, gate_ref, tbl_ref, out_ref, *scratch):
    tiles, ybuf = scratch[:-1], scratch[-1]
    row_masks = _row_masks()
    parity = _parity_mask()
    upper = lax.broadcasted_iota(jnp.int32, (SLAB_ROWS, HALF_D), 0) >= GATHER_NT

    def accumulate(t, n, tile, acc, reg):
        xrow = x_ref[pl.ds(t, 1), :]
        xs = jnp.where(upper, jnp.broadcast_to(xrow[:, HALF_D:], (SLAB_ROWS, HALF_D)),
                       jnp.broadcast_to(xrow[:, :HALF_D], (SLAB_ROWS, HALF_D)))
        lhs = (xs * row_masks[n][:, :1]).astype(_BF16)
        return _mxu_accumulate(acc, [lhs[:, :256], lhs[:, 256:]], tile, reg, transpose=True)

    def drain(tokens, acc):
        z0, z1 = _mxu_pop(acc)
        z = (z0 + z1) * parity
        ys = z[:GATHER_NT] + z[GATHER_NT:]
        for n, t in enumerate(tokens):
            ybuf[pl.ds(t, 1), :] = ys[n:n + 1, :]

    _pipelined_tokens(idx_ref, tbl_ref, tiles, accumulate, drain)
    y = ybuf[...]
    lane = lax.broadcasted_iota(jnp.int32, y.shape, 1)
    other = jnp.where((lane & 1) == 0, pltpu.roll(y, 2 * N_PAIRS - 1, axis=1), pltpu.roll(y, 1, axis=1))
    out_ref[...] = gate_ref[...] * _gelu(y + other)


def _v_pass_kernel(idx_ref, w_ref, tbl_ref, out_ref, *tiles):
    row_masks = _row_masks()
    parity = _parity_mask()

    def accumulate(t, n, tile, acc, reg):
        wrow = jnp.broadcast_to(w_ref[pl.ds(t, 1), :], (SLAB_ROWS, 2 * N_PAIRS))
        lhs = (wrow * (row_masks[n] * parity)).astype(_BF16)
        return _mxu_accumulate(acc, [lhs, lhs], tile, reg, transpose=False)

    def drain(tokens, acc):
        for h, r in enumerate(_mxu_pop(acc)):
            for b in range(2):
                for n, t in enumerate(tokens):
                    row = GATHER_NT * b + n
                    out_ref[pl.ds(t, 1), pl.ds(HALF_D * b + 256 * h, 256)] = r[row:row + 1, :]

    _pipelined_tokens(idx_ref, tbl_ref, tiles, accumulate, drain)


def _gather_call(kernel_fn, name, n_tok, in_specs, out_spec, out_shape, n_extra_scratch, args):
    tb = GATHER_TB
    tile = pltpu.VMEM((ROW_WORDS * N_PAIRS, 128), jnp.int32)
    return pl.pallas_call(
        kernel_fn,
        out_shape=out_shape,
        grid=(n_tok // tb,),
        in_specs=in_specs,
        out_specs=out_spec,
        scratch_shapes=[tile] * (2 * GATHER_NT) + n_extra_scratch,
        compiler_params=pltpu.CompilerParams(
            dimension_semantics=("arbitrary",),
            vmem_limit_bytes=GATHER_VMEM_BYTES,
        ),
        name=name,
    )(*args)


def _u_pass(idx, h2, gate2, tbl):
    n_tok = idx.shape[0]
    tb = GATHER_TB
    return _gather_call(
        _u_pass_kernel, "peer_u_pass", n_tok,
        [pl.BlockSpec((tb, N_PAIRS), lambda i: (i, 0), memory_space=pltpu.SMEM),
         pl.BlockSpec((tb, D_MODEL), lambda i: (i, 0)),
         pl.BlockSpec((tb, 2 * N_PAIRS), lambda i: (i, 0)),
         _resident()],
        pl.BlockSpec((tb, 2 * N_PAIRS), lambda i: (i, 0)),
        jax.ShapeDtypeStruct((n_tok, 2 * N_PAIRS), _F32),
        [pltpu.VMEM((tb, 2 * N_PAIRS), _F32)],
        (idx, h2, gate2, tbl))


def _v_pass(idx, w2, tbl):
    n_tok = idx.shape[0]
    tb = GATHER_TB
    return _gather_call(
        _v_pass_kernel, "peer_v_pass", n_tok,
        [pl.BlockSpec((tb, N_PAIRS), lambda i: (i, 0), memory_space=pltpu.SMEM),
         pl.BlockSpec((tb, 2 * N_PAIRS), lambda i: (i, 0)),
         _resident()],
        pl.BlockSpec((tb, D_MODEL), lambda i: (i, 0)),
        jax.ShapeDtypeStruct((n_tok, D_MODEL), _F32),
        [],
        (idx, w2, tbl))


def _final_kernel(x1_ref, ffn_ref, mod_ref, g_ref, b_ref, o_ref):
    g2 = mod_ref[0][5:6]
    o_ref[0] = _ln(DEEPNORM_ALPHA * x1_ref[0] + g2 * ffn_ref[0]) * g_ref[...] + b_ref[...]


def _final_norm(x1, ffn, mod3, g, b):
    bsz, seq, d = x1.shape
    ts = NORM_TS
    blk = pl.BlockSpec((1, ts, d), lambda i, s: (i, s, 0))
    vec = pl.BlockSpec((1, d), lambda i, s: (0, 0))
    return pl.pallas_call(
        _final_kernel,
        out_shape=jax.ShapeDtypeStruct((bsz, seq, d), _F32),
        grid=(bsz, seq // ts),
        in_specs=[blk, blk, pl.BlockSpec((1, 6, d), lambda i, s: (i, 0, 0)), vec, vec],
        out_specs=blk,
        compiler_params=pltpu.CompilerParams(dimension_semantics=("arbitrary", "arbitrary")),
        name="final_norm",
    )(x1, ffn, mod3, g, b)


def kernel(x, c, w_cond, b_cond, w_in, gmlp_ln_g, gmlp_ln_b, w_spatial, b_spatial, conv_w, p_a, p_b, w_o, ln1_g, ln1_b, w_q_peer, sub_keys, expert_u, expert_v, ln2_g, ln2_b):
    bsz, seq, d = x.shape
    n_tok = bsz * seq
    depth = w_cond.shape[0]
    row = lambda v: v.reshape(1, d)
    for l in range(depth):
        mod3 = _cond_proj(c, w_cond[l], b_cond[l]).reshape(bsz, 6, d)
        x1, h2 = _mixer(
            x, mod3, w_in[l].astype(_BF16), row(gmlp_ln_g[l]), row(gmlp_ln_b[l]),
            w_spatial[l].astype(_BF16), b_spatial[l][:, :, None], conv_w[l],
            p_a[l].astype(_BF16), p_b[l].astype(_BF16), w_o[l].astype(_BF16),
            row(ln1_g[l]), row(ln1_b[l]))
        h2 = h2.reshape(n_tok, d)
        keys = sub_keys[l].astype(_BF16).reshape(2 * PEER_HEADS, PEER_N_KEYS, PEER_HALF)
        idx, gate2 = _route(h2, w_q_peer[l].astype(_BF16), keys)
        w2 = _u_pass(idx, h2, gate2, _pack_table(expert_u[l]))
        ffn = _v_pass(idx, w2, _pack_table(expert_v[l])).reshape(bsz, seq, d)
        x = _final_norm(x1, ffn, mod3, row(ln2_g[l]), row(ln2_b[l]))
    return x
```

```python
import functools

import jax
import jax.numpy as jnp
from jax import lax
from jax.experimental import pallas as pl
from jax.experimental.pallas import tpu as pltpu

D_MODEL = 1024
CHUNK = 64
GMLP_BLOCK = 128
GMLP_GROUPS = 8
CONV_K = 3
PEER_HEADS = 8
PEER_HALF = 128
PEER_N_KEYS = 128
PEER_TOPK = 16
N_PAIRS = PEER_HEADS * PEER_TOPK
DEEPNORM_ALPHA = 2.0 ** 0.25
LN_EPS = 1e-5

ROW_WORDS = 4
HALF_D = D_MODEL // 2
MIX_TS = 512
ROUTE_TT = 256
GATHER_TB = 512
GATHER_NT = 8
SLAB_ROWS = 2 * GATHER_NT
NORM_TS = 512
PACK_TE = 512
MXU_ACC = (0, 4)
GATHER_VMEM_BYTES = 48 * 1024 * 1024
MIXER_VMEM_BYTES = 56 * 1024 * 1024

_F32 = jnp.float32
_BF16 = jnp.bfloat16
_NEG_INF = float("-inf")


def _ln(x):
    mu = jnp.mean(x, axis=-1, keepdims=True)
    xc = x - mu
    var = jnp.mean(xc * xc, axis=-1, keepdims=True)
    return xc * lax.rsqrt(var + LN_EPS)


def _gelu(x):
    return 0.5 * x * (1.0 + jnp.tanh(0.7978845608028654 * (x + 0.044715 * (x * x * x))))


def _sigmoid(x):
    return 1.0 / (1.0 + jnp.exp(-x))


def _resident():
    return pl.BlockSpec(memory_space=pltpu.VMEM)


def _cond_kernel(c_ref, w_ref, b_ref, o_ref):
    c = c_ref[...]
    a = (c * _sigmoid(c)).astype(_BF16)
    o_ref[...] = jnp.dot(a, w_ref[...].astype(_BF16), preferred_element_type=_F32) + b_ref[...]


def _cond_proj(c, w, b):
    bsz, d = c.shape
    n = w.shape[1]
    tn = 1024
    return pl.pallas_call(
        _cond_kernel,
        out_shape=jax.ShapeDtypeStruct((bsz, n), _F32),
        grid=(n // tn,),
        in_specs=[
            pl.BlockSpec((bsz, d), lambda j: (0, 0)),
            pl.BlockSpec((d, tn), lambda j: (0, j)),
            pl.BlockSpec((1, tn), lambda j: (0, j)),
        ],
        out_specs=pl.BlockSpec((bsz, tn), lambda j: (0, j)),
        compiler_params=pltpu.CompilerParams(dimension_semantics=("arbitrary",)),
        name="cond_proj",
    )(c, w, b.reshape(1, n))


def _mixer_kernel(x_ref, mod_ref, w_in_ref, gg_ref, gb_ref, ws_ref, bs_ref, cw_ref, pa_ref, pb_ref,
                  wo_ref, l1g_ref, l1b_ref, x1_ref, h2_ref, prev_ref, gu_ref, vn_ref, ya_ref):
    ts = x_ref.shape[1]
    d = D_MODEL

    @pl.when(pl.program_id(1) == 0)
    def _():
        prev_ref[...] = jnp.zeros_like(prev_ref)

    x = x_ref[0]
    mod = mod_ref[0]
    sh1, sc1, g1, sh2, sc2 = (mod[i:i + 1] for i in range(5))
    h = (_ln(x) * (1.0 + sc1) + sh1).astype(_BF16)

    def proj(j):
        return jnp.dot(h, w_in_ref[:, j * d:(j + 1) * d], preferred_element_type=_F32)

    gu_ref[...] = _gelu(proj(0))
    vn_ref[...] = (_ln(_gelu(proj(1))) * gg_ref[...] + gb_ref[...]).astype(_BF16)
    qi = lax.broadcasted_iota(jnp.int32, (GMLP_BLOCK, GMLP_BLOCK), 0) // CHUNK
    kj = lax.broadcasted_iota(jnp.int32, (GMLP_BLOCK, GMLP_BLOCK), 1) // CHUNK
    causal = kj <= qi
    for g in range(GMLP_GROUPS):
        wg = jnp.where(causal, ws_ref[g], jnp.zeros((), _BF16))
        cols = slice(g * GMLP_BLOCK, (g + 1) * GMLP_BLOCK)
        for n in range(ts // GMLP_BLOCK):
            rows = slice(n * GMLP_BLOCK, (n + 1) * GMLP_BLOCK)
            mixed = jnp.dot(wg, vn_ref[rows, cols], preferred_element_type=_F32) + bs_ref[g]
            ya_ref[rows, cols] = (gu_ref[rows, cols] * mixed).astype(_BF16)

    g_b = proj(2)
    zc = proj(3) * proj(4)
    ext = jnp.concatenate([prev_ref[...], zc], axis=0)
    prev_ref[...] = zc[ts - 8:, :]
    cw = cw_ref[...]
    y = ext[6:ts + 6, :] * cw[0:1] + ext[7:ts + 7, :] * cw[1:2] + zc * cw[2:3]
    y_b = (g_b * y).astype(_BF16)

    merged = (_sigmoid(proj(5)) * jnp.dot(ya_ref[...], pa_ref[...], preferred_element_type=_F32)
              + _sigmoid(proj(6)) * jnp.dot(y_b, pb_ref[...], preferred_element_type=_F32))
    mix = jnp.dot(merged.astype(_BF16), wo_ref[...], preferred_element_type=_F32)
    x1 = _ln(DEEPNORM_ALPHA * x + g1 * mix) * l1g_ref[...] + l1b_ref[...]
    x1_ref[0] = x1
    h2_ref[0] = _ln(x1) * (1.0 + sc2) + sh2


def _mixer(x, mod3, w_in, gg, gb, ws, bs, cw, pa, pb, wo, l1g, l1b):
    bsz, seq, d = x.shape
    ts = MIX_TS
    blk = pl.BlockSpec((1, ts, d), lambda b, s: (b, s, 0))
    return pl.pallas_call(
        _mixer_kernel,
        out_shape=(jax.ShapeDtypeStruct((bsz, seq, d), _F32),
                   jax.ShapeDtypeStruct((bsz, seq, d), _F32)),
        grid=(bsz, seq // ts),
        in_specs=[blk, pl.BlockSpec((1, 6, d), lambda b, s: (b, 0, 0))] + [_resident()] * 11,
        out_specs=(blk, blk),
        scratch_shapes=[
            pltpu.VMEM((8, d), _F32),
            pltpu.VMEM((ts, d), _F32),
            pltpu.VMEM((ts, d), _BF16),
            pltpu.VMEM((ts, d), _BF16),
        ],
        compiler_params=pltpu.CompilerParams(
            dimension_semantics=("arbitrary", "arbitrary"),
            vmem_limit_bytes=MIXER_VMEM_BYTES,
        ),
        name="mixer",
    )(x, mod3, w_in, gg, gb, ws, bs, cw, pa, pb, wo, l1g, l1b)


def _sorted_top(vals, ids, payload, k):
    vals, ids = list(vals), list(ids)
    pay = None if payload is None else list(payload)
    g = len(vals)
    for phase in range(g):
        for i in range(phase % 2, g - 1, 2):
            swap = vals[i + 1] > vals[i]
            vals[i], vals[i + 1] = jnp.maximum(vals[i], vals[i + 1]), jnp.minimum(vals[i], vals[i + 1])
            ids[i], ids[i + 1] = jnp.where(swap, ids[i + 1], ids[i]), jnp.where(swap, ids[i], ids[i + 1])
            if pay is not None:
                pay[i], pay[i + 1] = jnp.where(swap, pay[i + 1], pay[i]), jnp.where(swap, pay[i], pay[i + 1])
    big = jnp.int32(2 ** 30)
    top_v, top_p = [], []
    for r in range(k):
        m = jnp.max(vals[0], axis=0, keepdims=True)
        sel = jnp.min(jnp.where(vals[0] == m, ids[0], big), axis=0, keepdims=True)
        hit = ids[0] == sel
        top_v.append(m)
        top_p.append(sel if pay is None else jnp.max(jnp.where(hit, pay[0], -1), axis=0, keepdims=True))
        for i in range(min(g, k - r - 1)):
            nxt = i + 1 < g
            vals[i] = jnp.where(hit, vals[i + 1] if nxt else _NEG_INF, vals[i])
            ids[i] = jnp.where(hit, ids[i + 1] if nxt else big, ids[i])
            if pay is not None and nxt:
                pay[i] = jnp.where(hit, pay[i + 1], pay[i])
    return top_v, top_p


def _route_kernel(h_ref, wq_ref, keys_ref, idx_ref, gate_ref, st_ref, it_ref, idx_t_ref, gate_t_ref):
    tt = h_ref.shape[0]
    lanes = 128
    q = jnp.dot(h_ref[...].astype(_BF16), wq_ref[...], preferred_element_type=_F32).astype(_BF16)
    row = lax.broadcasted_iota(jnp.int32, (8, lanes), 0)
    groups = PEER_N_KEYS // 8
    for hp in range(2 * PEER_HEADS):
        s = lax.dot_general(keys_ref[hp], q[:, hp * PEER_HALF:(hp + 1) * PEER_HALF],
                            (((1,), (1,)), ((), ())), preferred_element_type=_F32)
        for c in range(tt // lanes):
            cols = slice(c * lanes, (c + 1) * lanes)
            top_v, top_i = _sorted_top([s[8 * i:8 * i + 8, cols] for i in range(groups)],
                                       [row + 8 * i for i in range(groups)], None, PEER_TOPK)
            for r in range(PEER_TOPK):
                st_ref[hp, r:r + 1, cols] = top_v[r]
                it_ref[hp, r:r + 1, cols] = top_i[r]

    for hd in range(PEER_HEADS):
        for c in range(tt // lanes):
            cols = slice(c * lanes, (c + 1) * lanes)
            s1, s2 = st_ref[2 * hd, :, cols], st_ref[2 * hd + 1, :, cols]
            i1, i2 = it_ref[2 * hd, :, cols], it_ref[2 * hd + 1, :, cols]
            vals, eids, cids = [], [], []
            for a in range(8):
                vals.append(s1[a:a + 1] + s2[0:8])
                eids.append(i1[a:a + 1] * PEER_N_KEYS + i2[0:8])
                cids.append(a * PEER_TOPK + row)
                if a == 0:
                    vals.append(s1[0:1] + s2[8:16])
                    eids.append(i1[0:1] * PEER_N_KEYS + i2[8:16])
                    cids.append(8 + row)
            vals.append(s1[8:16] + s2[0:1])
            eids.append(i1[8:16] * PEER_N_KEYS + i2[0:1])
            cids.append((8 + row) * PEER_TOPK)
            top_s, top_e = _sorted_top(vals, cids, eids, PEER_TOPK)
            ex = [jnp.exp(v - top_s[0]) for v in top_s]
            denom = functools.reduce(lambda p, r: p + r, ex)
            for r in range(PEER_TOPK):
                k = hd * PEER_TOPK + r
                idx_t_ref[k:k + 1, cols] = top_e[r] * ROW_WORDS
                gate_t_ref[2 * k:2 * k + 2, cols] = jnp.broadcast_to(ex[r] / denom, (2, lanes))
    idx_ref[...] = idx_t_ref[...].T
    gate_ref[...] = gate_t_ref[...].T


def _route(h2, wq, keys):
    n_tok, d = h2.shape
    tt = ROUTE_TT
    return pl.pallas_call(
        _route_kernel,
        out_shape=(jax.ShapeDtypeStruct((n_tok, N_PAIRS), jnp.int32),
                   jax.ShapeDtypeStruct((n_tok, 2 * N_PAIRS), _F32)),
        grid=(n_tok // tt,),
        in_specs=[pl.BlockSpec((tt, d), lambda i: (i, 0)), _resident(), _resident()],
        out_specs=(pl.BlockSpec((tt, N_PAIRS), lambda i: (i, 0)),
                   pl.BlockSpec((tt, 2 * N_PAIRS), lambda i: (i, 0))),
        scratch_shapes=[
            pltpu.VMEM((2 * PEER_HEADS, PEER_TOPK, tt), _F32),
            pltpu.VMEM((2 * PEER_HEADS, PEER_TOPK, tt), jnp.int32),
            pltpu.VMEM((N_PAIRS, tt), jnp.int32),
            pltpu.VMEM((2 * N_PAIRS, tt), _F32),
        ],
        compiler_params=pltpu.CompilerParams(dimension_semantics=("arbitrary",)),
        name="peer_route",
    )(h2, wq, keys)


def _pack_kernel(w_ref, o_ref):
    n = w_ref.shape[0]
    w = w_ref[...]
    lo = pltpu.bitcast(w[:, :HALF_D].astype(_BF16).astype(_F32), jnp.uint32) >> 16
    hi = pltpu.bitcast(w[:, HALF_D:].astype(_BF16).astype(_F32), jnp.uint32) & jnp.uint32(0xFFFF0000)
    words = pltpu.bitcast(lo | hi, jnp.int32)
    for r in range(ROW_WORDS):
        o_ref[pl.ds(r, n, stride=ROW_WORDS), :] = words[:, 128 * r:128 * (r + 1)]


def _pack_table(w):
    n, d = w.shape
    te = PACK_TE
    return pl.pallas_call(
        _pack_kernel,
        out_shape=jax.ShapeDtypeStruct((n * ROW_WORDS, 128), jnp.int32),
        grid=(n // te,),
        in_specs=[pl.BlockSpec((te, d), lambda i: (i, 0))],
        out_specs=pl.BlockSpec((te * ROW_WORDS, 128), lambda i: (i, 0)),
        compiler_params=pltpu.CompilerParams(dimension_semantics=("arbitrary",)),
        name="pack_table",
    )(w)


def _gather_rows(idx_ref, tokens, tbl_ref, slab_tiles, lo=0, hi=N_PAIRS):
    rows = [idx_ref.at[t] for t in tokens]
    for k in range(lo, hi):
        for row, tile in zip(rows, slab_tiles):
            i = pl.multiple_of(row[k], ROW_WORDS)
            tile[pl.ds(ROW_WORDS * k, ROW_WORDS), :] = tbl_ref[pl.ds(i, ROW_WORDS), :]


def _rhs_half(tile, h):
    parts = [pltpu.bitcast(tile[pl.ds(2 * h + r, N_PAIRS, stride=ROW_WORDS), :], _BF16)
             for r in range(2)]
    return jnp.concatenate(parts, axis=1)


def _mxu_accumulate(acc, lhs_halves, tile, reg, transpose):
    def step(h):
        pltpu.matmul_push_rhs(_rhs_half(tile, h), staging_register=reg, mxu_index=h, transpose=transpose)
        pltpu.matmul_acc_lhs(acc, lhs_halves[h], mxu_index=h, load_staged_rhs=reg)

    return [functools.partial(step, h) for h in range(2)]


def _mxu_pop(acc):
    return [pltpu.matmul_pop(acc, (SLAB_ROWS, 256), _F32, mxu_index=h) for h in range(2)]


def _row_masks():
    sub = lax.broadcasted_iota(jnp.int32, (SLAB_ROWS, 2 * N_PAIRS), 0)
    return [((sub % GATHER_NT) == n).astype(_F32) for n in range(GATHER_NT)]


def _parity_mask():
    lane = lax.broadcasted_iota(jnp.int32, (SLAB_ROWS, 2 * N_PAIRS), 1)
    sub = lax.broadcasted_iota(jnp.int32, (SLAB_ROWS, 2 * N_PAIRS), 0)
    return ((sub // GATHER_NT) == (lane & 1)).astype(_F32)


def _pipelined_tokens(idx_ref, tbl_ref, tiles, accumulate, drain):
    tb = idx_ref.shape[0]
    n_stage = tb // GATHER_NT
    sets = (tiles[:GATHER_NT], tiles[GATHER_NT:])
    slab_tokens = lambda j: [n * n_stage + j for n in range(GATHER_NT)]

    def stage(j, cur, nxt, acc):
        tokens = slab_tokens(j)
        drain(slab_tokens(jnp.maximum(j - 2, 0)), acc)
        steps = []
        for n in range(GATHER_NT):
            steps += accumulate(tokens[n], n, cur[n], acc, n % 2)
        ahead = slab_tokens(jnp.minimum(j + 1, n_stage - 1))
        part = N_PAIRS // len(steps)
        for i, step in enumerate(steps):
            step()
            _gather_rows(idx_ref, ahead, tbl_ref, nxt, part * i, part * (i + 1))

    def body(j, carry):
        for p in range(2):
            @pl.when((j & 1) == p)
            def _():
                stage(j, sets[p], sets[1 - p], MXU_ACC[p])

        return carry

    @pl.when(pl.program_id(0) == 0)
    def _():
        for acc in MXU_ACC:
            _mxu_pop(acc)

    _gather_rows(idx_ref, slab_tokens(0), tbl_ref, sets[0])
    lax.fori_loop(0, n_stage, body, 0)
    for j in (n_stage - 2, n_stage - 1):
        drain(slab_tokens(j), MXU_ACC[j % 2])


def _u_pass_kernel(idx_ref, x_ref, gate_ref, tbl_ref, out_ref, *scratch):
    tiles, ybuf = scratch[:-1], scratch[-1]
    row_masks = _row_masks()
    parity = _parity_mask()
    upper = lax.broadcasted_iota(jnp.int32, (SLAB_ROWS, HALF_D), 0) >= GATHER_NT

    def accumulate(t, n, tile, acc, reg):
        xrow = x_ref[pl.ds(t, 1), :]
        xs = jnp.where(upper, jnp.broadcast_to(xrow[:, HALF_D:], (SLAB_ROWS, HALF_D)),
                       jnp.broadcast_to(xrow[:, :HALF_D], (SLAB_ROWS, HALF_D)))
        lhs = (xs * row_masks[n][:, :1]).astype(_BF16)
        return _mxu_accumulate(acc, [lhs[:, :256], lhs[:, 256:]], tile, reg, transpose=True)

    def drain(tokens, acc):
        z0, z1 = _mxu_pop(acc)
        z = (z0 + z1) * parity
        ys = z[:GATHER_NT] + z[GATHER_NT:]
        for n, t in enumerate(tokens):
            ybuf[pl.ds(t, 1), :] = ys[n:n + 1, :]

    _pipelined_tokens(idx_ref, tbl_ref, tiles, accumulate, drain)
    y = ybuf[...]
    lane = lax.broadcasted_iota(jnp.int32, y.shape, 1)
    other = jnp.where((lane & 1) == 0, pltpu.roll(y, 2 * N_PAIRS - 1, axis=1), pltpu.roll(y, 1, axis=1))
    out_ref[...] = gate_ref[...] * _gelu(y + other)


def _v_pass_kernel(idx_ref, w_ref, tbl_ref, out_ref, *tiles):
    row_masks = _row_masks()
    parity = _parity_mask()

    def accumulate(t, n, tile, acc, reg):
        wrow = jnp.broadcast_to(w_ref[pl.ds(t, 1), :], (SLAB_ROWS, 2 * N_PAIRS))
        lhs = (wrow * (row_masks[n] * parity)).astype(_BF16)
        return _mxu_accumulate(acc, [lhs, lhs], tile, reg, transpose=False)

    def drain(tokens, acc):
        for h, r in enumerate(_mxu_pop(acc)):
            for b in range(2):
                for n, t in enumerate(tokens):
                    row = GATHER_NT * b + n
                    out_ref[pl.ds(t, 1), pl.ds(HALF_D * b + 256 * h, 256)] = r[row:row + 1, :]

    _pipelined_tokens(idx_ref, tbl_ref, tiles, accumulate, drain)


def _gather_call(kernel_fn, name, n_tok, in_specs, out_spec, out_shape, n_extra_scratch, args):
    tb = GATHER_TB
    tile = pltpu.VMEM((ROW_WORDS * N_PAIRS, 128), jnp.int32)
    return pl.pallas_call(
        kernel_fn,
        out_shape=out_shape,
        grid=(n_tok // tb,),
        in_specs=in_specs,
        out_specs=out_spec,
        scratch_shapes=[tile] * (2 * GATHER_NT) + n_extra_scratch,
        compiler_params=pltpu.CompilerParams(
            dimension_semantics=("arbitrary",),
            vmem_limit_bytes=GATHER_VMEM_BYTES,
        ),
        name=name,
    )(*args)


def _u_pass(idx, h2, gate2, tbl):
    n_tok = idx.shape[0]
    tb = GATHER_TB
    return _gather_call(
        _u_pass_kernel, "peer_u_pass", n_tok,
        [pl.BlockSpec((tb, N_PAIRS), lambda i: (i, 0), memory_space=pltpu.SMEM),
         pl.BlockSpec((tb, D_MODEL), lambda i: (i, 0)),
         pl.BlockSpec((tb, 2 * N_PAIRS), lambda i: (i, 0)),
         _resident()],
        pl.BlockSpec((tb, 2 * N_PAIRS), lambda i: (i, 0)),
        jax.ShapeDtypeStruct((n_tok, 2 * N_PAIRS), _F32),
        [pltpu.VMEM((tb, 2 * N_PAIRS), _F32)],
        (idx, h2, gate2, tbl))


def _v_pass(idx, w2, tbl):
    n_tok = idx.shape[0]
    tb = GATHER_TB
    return _gather_call(
        _v_pass_kernel, "peer_v_pass", n_tok,
        [pl.BlockSpec((tb, N_PAIRS), lambda i: (i, 0), memory_space=pltpu.SMEM),
         pl.BlockSpec((tb, 2 * N_PAIRS), lambda i: (i, 0)),
         _resident()],
        pl.BlockSpec((tb, D_MODEL), lambda i: (i, 0)),
        jax.ShapeDtypeStruct((n_tok, D_MODEL), _F32),
        [],
        (idx, w2, tbl))


def _final_kernel(x1_ref, ffn_ref, mod_ref, g_ref, b_ref, o_ref):
    g2 = mod_ref[0][5:6]
    o_ref[0] = _ln(DEEPNORM_ALPHA * x1_ref[0] + g2 * ffn_ref[0]) * g_ref[...] + b_ref[...]


def _final_norm(x1, ffn, mod3, g, b):
    bsz, seq, d = x1.shape
    ts = NORM_TS
    blk = pl.BlockSpec((1, ts, d), lambda i, s: (i, s, 0))
    vec = pl.BlockSpec((1, d), lambda i, s: (0, 0))
    return pl.pallas_call(
        _final_kernel,
        out_shape=jax.ShapeDtypeStruct((bsz, seq, d), _F32),
        grid=(bsz, seq // ts),
        in_specs=[blk, blk, pl.BlockSpec((1, 6, d), lambda i, s: (i, 0, 0)), vec, vec],
        out_specs=blk,
        compiler_params=pltpu.CompilerParams(dimension_semantics=("arbitrary", "arbitrary")),
        name="final_norm",
    )(x1, ffn, mod3, g, b)


def kernel(x, c, w_cond, b_cond, w_in, gmlp_ln_g, gmlp_ln_b, w_spatial, b_spatial, conv_w, p_a, p_b, w_o, ln1_g, ln1_b, w_q_peer, sub_keys, expert_u, expert_v, ln2_g, ln2_b):
    bsz, seq, d = x.shape
    n_tok = bsz * seq
    depth = w_cond.shape[0]
    row = lambda v: v.reshape(1, d)
    for l in range(depth):
        mod3 = _cond_proj(c, w_cond[l], b_cond[l]).reshape(bsz, 6, d)
        x1, h2 = _mixer(
            x, mod3, w_in[l].astype(_BF16), row(gmlp_ln_g[l]), row(gmlp_ln_b[l]),
            w_spatial[l].astype(_BF16), b_spatial[l][:, :, None], conv_w[l],
            p_a[l].astype(_BF16), p_b[l].astype(_BF16), w_o[l].astype(_BF16),
            row(ln1_g[l]), row(ln1_b[l]))
        h2 = h2.reshape(n_tok, d)
        keys = sub_keys[l].astype(_BF16).reshape(2 * PEER_HEADS, PEER_N_KEYS, PEER_HALF)
        idx, gate2 = _route(h2, w_q_peer[l].astype(_BF16), keys)
        w2 = _u_pass(idx, h2, gate2, _pack_table(expert_u[l]))
        ffn = _v_pass(idx, w2, _pack_table(expert_v[l])).reshape(bsz, seq, d)
        x = _final_norm(x1, ffn, mod3, row(ln2_g[l]), row(ln2_b[l]))
    return x
```

```python
import functools

import jax
import jax.numpy as jnp
from jax import lax
from jax.experimental import pallas as pl
from jax.experimental.pallas import tpu as pltpu

D_MODEL = 1024
CHUNK = 64
GMLP_BLOCK = 128
GMLP_GROUPS = 8
CONV_K = 3
PEER_HEADS = 8
PEER_HALF = 128
PEER_N_KEYS = 128
PEER_TOPK = 16
N_PAIRS = PEER_HEADS * PEER_TOPK
DEEPNORM_ALPHA = 2.0 ** 0.25
LN_EPS = 1e-5

ROW_WORDS = 4
HALF_D = D_MODEL // 2
MIX_TS = 512
ROUTE_TT = 256
GATHER_TB = 512
GATHER_NT = 8
SLAB_ROWS = 2 * GATHER_NT
NORM_TS = 512
PACK_TE = 512
MXU_ACC = (0, 4)
GATHER_VMEM_BYTES = 48 * 1024 * 1024
MIXER_VMEM_BYTES = 56 * 1024 * 1024

_F32 = jnp.float32
_BF16 = jnp.bfloat16
_NEG_INF = float("-inf")


def _ln(x):
    mu = jnp.mean(x, axis=-1, keepdims=True)
    xc = x - mu
    var = jnp.mean(xc * xc, axis=-1, keepdims=True)
    return xc * lax.rsqrt(var + LN_EPS)


def _gelu(x):
    return 0.5 * x * (1.0 + jnp.tanh(0.7978845608028654 * (x + 0.044715 * (x * x * x))))


def _sigmoid(x):
    return 1.0 / (1.0 + jnp.exp(-x))


def _resident():
    return pl.BlockSpec(memory_space=pltpu.VMEM)


def _cond_kernel(c_ref, w_ref, b_ref, o_ref):
    c = c_ref[...]
    a = (c * _sigmoid(c)).astype(_BF16)
    o_ref[...] = jnp.dot(a, w_ref[...].astype(_BF16), preferred_element_type=_F32) + b_ref[...]


def _cond_proj(c, w, b):
    bsz, d = c.shape
    n = w.shape[1]
    tn = 1024
    return pl.pallas_call(
        _cond_kernel,
        out_shape=jax.ShapeDtypeStruct((bsz, n), _F32),
        grid=(n // tn,),
        in_specs=[
            pl.BlockSpec((bsz, d), lambda j: (0, 0)),
            pl.BlockSpec((d, tn), lambda j: (0, j)),
            pl.BlockSpec((1, tn), lambda j: (0, j)),
        ],
        out_specs=pl.BlockSpec((bsz, tn), lambda j: (0, j)),
        compiler_params=pltpu.CompilerParams(dimension_semantics=("arbitrary",)),
        name="cond_proj",
    )(c, w, b.reshape(1, n))


def _mixer_kernel(x_ref, mod_ref, w_in_ref, gg_ref, gb_ref, ws_ref, bs_ref, cw_ref, pa_ref, pb_ref,
                  wo_ref, l1g_ref, l1b_ref, x1_ref, h2_ref, prev_ref, gu_ref, vn_ref, ya_ref):
    ts = x_ref.shape[1]
    d = D_MODEL

    @pl.when(pl.program_id(1) == 0)
    def _():
        prev_ref[...] = jnp.zeros_like(prev_ref)

    x = x_ref[0]
    mod = mod_ref[0]
    sh1, sc1, g1, sh2, sc2 = (mod[i:i + 1] for i in range(5))
    h = (_ln(x) * (1.0 + sc1) + sh1).astype(_BF16)

    def proj(j):
        return jnp.dot(h, w_in_ref[:, j * d:(j + 1) * d], preferred_element_type=_F32)

    gu_ref[...] = _gelu(proj(0))
    vn_ref[...] = (_ln(_gelu(proj(1))) * gg_ref[...] + gb_ref[...]).astype(_BF16)
    qi = lax.broadcasted_iota(jnp.int32, (GMLP_BLOCK, GMLP_BLOCK), 0) // CHUNK
    kj = lax.broadcasted_iota(jnp.int32, (GMLP_BLOCK, GMLP_BLOCK), 1) // CHUNK
    causal = kj <= qi
    for g in range(GMLP_GROUPS):
        wg = jnp.where(causal, ws_ref[g], jnp.zeros((), _BF16))
        cols = slice(g * GMLP_BLOCK, (g + 1) * GMLP_BLOCK)
        for n in range(ts // GMLP_BLOCK):
            rows = slice(n * GMLP_BLOCK, (n + 1) * GMLP_BLOCK)
            mixed = jnp.dot(wg, vn_ref[rows, cols], preferred_element_type=_F32) + bs_ref[g]
            ya_ref[rows, cols] = (gu_ref[rows, cols] * mixed).astype(_BF16)

    g_b = proj(2)
    zc = proj(3) * proj(4)
    ext = jnp.concatenate([prev_ref[...], zc], axis=0)
    prev_ref[...] = zc[ts - 8:, :]
    cw = cw_ref[...]
    y = ext[6:ts + 6, :] * cw[0:1] + ext[7:ts + 7, :] * cw[1:2] + zc * cw[2:3]
    y_b = (g_b * y).astype(_BF16)

    merged = (_sigmoid(proj(5)) * jnp.dot(ya_ref[...], pa_ref[...], preferred_element_type=_F32)
              + _sigmoid(proj(6)) * jnp.dot(y_b, pb_ref[...], preferred_element_type=_F32))
    mix = jnp.dot(merged.astype(_BF16), wo_ref[...], preferred_element_type=_F32)
    x1 = _ln(DEEPNORM_ALPHA * x + g1 * mix) * l1g_ref[...] + l1b_ref[...]
    x1_ref[0] = x1
    h2_ref[0] = _ln(x1) * (1.0 + sc2) + sh2


def _mixer(x, mod3, w_in, gg, gb, ws, bs, cw, pa, pb, wo, l1g, l1b):
    bsz, seq, d = x.shape
    ts = MIX_TS
    blk = pl.BlockSpec((1, ts, d), lambda b, s: (b, s, 0))
    return pl.pallas_call(
        _mixer_kernel,
        out_shape=(jax.ShapeDtypeStruct((bsz, seq, d), _F32),
                   jax.ShapeDtypeStruct((bsz, seq, d), _F32)),
        grid=(bsz, seq // ts),
        in_specs=[blk, pl.BlockSpec((1, 6, d), lambda b, s: (b, 0, 0))] + [_resident()] * 11,
        out_specs=(blk, blk),
        scratch_shapes=[
            pltpu.VMEM((8, d), _F32),
            pltpu.VMEM((ts, d), _F32),
            pltpu.VMEM((ts, d), _BF16),
            pltpu.VMEM((ts, d), _BF16),
        ],
        compiler_params=pltpu.CompilerParams(
            dimension_semantics=("arbitrary", "arbitrary"),
            vmem_limit_bytes=MIXER_VMEM_BYTES,
        ),
        name="mixer",
    )(x, mod3, w_in, gg, gb, ws, bs, cw, pa, pb, wo, l1g, l1b)


def _sorted_top(vals, ids, payload, k):
    vals, ids = list(vals), list(ids)
    pay = None if payload is None else list(payload)
    g = len(vals)
    for phase in range(g):
        for i in range(phase % 2, g - 1, 2):
            swap = vals[i + 1] > vals[i]
            vals[i], vals[i + 1] = jnp.maximum(vals[i], vals[i + 1]), jnp.minimum(vals[i], vals[i + 1])
            ids[i], ids[i + 1] = jnp.where(swap, ids[i + 1], ids[i]), jnp.where(swap, ids[i], ids[i + 1])
            if pay is not None:
                pay[i], pay[i + 1] = jnp.where(swap, pay[i + 1], pay[i]), jnp.where(swap, pay[i], pay[i + 1])
    big = jnp.int32(2 ** 30)
    top_v, top_p = [], []
    for r in range(k):
        m = jnp.max(vals[0], axis=0, keepdims=True)
        sel = jnp.min(jnp.where(vals[0] == m, ids[0], big), axis=0, keepdims=True)
        hit = ids[0] == sel
        top_v.append(m)
        top_p.append(sel if pay is None else jnp.max(jnp.where(hit, pay[0], -1), axis=0, keepdims=True))
        for i in range(min(g, k - r - 1)):
            nxt = i + 1 < g
            vals[i] = jnp.where(hit, vals[i + 1] if nxt else _NEG_INF, vals[i])
            ids[i] = jnp.where(hit, ids[i + 1] if nxt else big, ids[i])
            if pay is not None and nxt:
                pay[i] = jnp.where(hit, pay[i + 1], pay[i])
    return top_v, top_p


def _route_kernel(h_ref, wq_ref, keys_ref, idx_ref, gate_ref, st_ref, it_ref, idx_t_ref, gate_t_ref):
    tt = h_ref.shape[0]
    lanes = 128
    q = jnp.dot(h_ref[...].astype(_BF16), wq_ref[...], preferred_element_type=_F32).astype(_BF16)
    row = lax.broadcasted_iota(jnp.int32, (8, lanes), 0)
    groups = PEER_N_KEYS // 8
    for hp in range(2 * PEER_HEADS):
        s = lax.dot_general(keys_ref[hp], q[:, hp * PEER_HALF:(hp + 1) * PEER_HALF],
                            (((1,), (1,)), ((), ())), preferred_element_type=_F32)
        for c in range(tt // lanes):
            cols = slice(c * lanes, (c + 1) * lanes)
            top_v, top_i = _sorted_top([s[8 * i:8 * i + 8, cols] for i in range(groups)],
                                       [row + 8 * i for i in range(groups)], None, PEER_TOPK)
            for r in range(PEER_TOPK):
                st_ref[hp, r:r + 1, cols] = top_v[r]
                it_ref[hp, r:r + 1, cols] = top_i[r]

    for hd in range(PEER_HEADS):
        for c in range(tt // lanes):
            cols = slice(c * lanes, (c + 1) * lanes)
            s1, s2 = st_ref[2 * hd, :, cols], st_ref[2 * hd + 1, :, cols]
            i1, i2 = it_ref[2 * hd, :, cols], it_ref[2 * hd + 1, :, cols]
            vals, eids, cids = [], [], []
            for a in range(8):
                vals.append(s1[a:a + 1] + s2[0:8])
                eids.append(i1[a:a + 1] * PEER_N_KEYS + i2[0:8])
                cids.append(a * PEER_TOPK + row)
                if a == 0:
                    vals.append(s1[0:1] + s2[8:16])
                    eids.append(i1[0:1] * PEER_N_KEYS + i2[8:16])
                    cids.append(8 + row)
            vals.append(s1[8:16] + s2[0:1])
            eids.append(i1[8:16] * PEER_N_KEYS + i2[0:1])
            cids.append((8 + row) * PEER_TOPK)
            top_s, top_e = _sorted_top(vals, cids, eids, PEER_TOPK)
            ex = [jnp.exp(v - top_s[0]) for v in top_s]
            denom = functools.reduce(lambda p, r: p + r, ex)
            for r in range(PEER_TOPK):
                k = hd * PEER_TOPK + r
                idx_t_ref[k:k + 1, cols] = top_e[r] * ROW_WORDS
                gate_t_ref[2 * k:2 * k + 2, cols] = jnp.broadcast_to(ex[r] / denom, (2, lanes))
    idx_ref[...] = idx_t_ref[...].T
    gate_ref[...] = gate_t_ref[...].T


def _route(h2, wq, keys):
    n_tok, d = h2.shape
    tt = ROUTE_TT
    return pl.pallas_call(
        _route_kernel,
        out_shape=(jax.ShapeDtypeStruct((n_tok, N_PAIRS), jnp.int32),
                   jax.ShapeDtypeStruct((n_tok, 2 * N_PAIRS), _F32)),
        grid=(n_tok // tt,),
        in_specs=[pl.BlockSpec((tt, d), lambda i: (i, 0)), _resident(), _resident()],
        out_specs=(pl.BlockSpec((tt, N_PAIRS), lambda i: (i, 0)),
                   pl.BlockSpec((tt, 2 * N_PAIRS), lambda i: (i, 0))),
        scratch_shapes=[
            pltpu.VMEM((2 * PEER_HEADS, PEER_TOPK, tt), _F32),
            pltpu.VMEM((2 * PEER_HEADS, PEER_TOPK, tt), jnp.int32),
            pltpu.VMEM((N_PAIRS, tt), jnp.int32),
            pltpu.VMEM((2 * N_PAIRS, tt), _F32),
        ],
        compiler_params=pltpu.CompilerParams(dimension_semantics=("arbitrary",)),
        name="peer_route",
    )(h2, wq, keys)


def _pack_kernel(w_ref, o_ref):
    n = w_ref.shape[0]
    w = w_ref[...]
    lo = pltpu.bitcast(w[:, :HALF_D].astype(_BF16).astype(_F32), jnp.uint32) >> 16
    hi = pltpu.bitcast(w[:, HALF_D:].astype(_BF16).astype(_F32), jnp.uint32) & jnp.uint32(0xFFFF0000)
    words = pltpu.bitcast(lo | hi, jnp.int32)
    for r in range(ROW_WORDS):
        o_ref[pl.ds(r, n, stride=ROW_WORDS), :] = words[:, 128 * r:128 * (r + 1)]


def _pack_table(w):
    n, d = w.shape
    te = PACK_TE
    return pl.pallas_call(
        _pack_kernel,
        out_shape=jax.ShapeDtypeStruct((n * ROW_WORDS, 128), jnp.int32),
        grid=(n // te,),
        in_specs=[pl.BlockSpec((te, d), lambda i: (i, 0))],
        out_specs=pl.BlockSpec((te * ROW_WORDS, 128), lambda i: (i, 0)),
        compiler_params=pltpu.CompilerParams(dimension_semantics=("arbitrary",)),
        name="pack_table",
    )(w)


def _gather_rows(idx_ref, tokens, tbl_ref, slab_tiles, lo=0, hi=N_PAIRS):
    rows = [idx_ref.at[t] for t in tokens]
    for k in range(lo, hi):
        for row, tile in zip(rows, slab_tiles):
            i = pl.multiple_of(row[k], ROW_WORDS)
            tile[pl.ds(ROW_WORDS * k, ROW_WORDS), :] = tbl_ref[pl.ds(i, ROW_WORDS), :]


def _rhs_half(tile, h):
    parts = [pltpu.bitcast(tile[pl.ds(2 * h + r, N_PAIRS, stride=ROW_WORDS), :], _BF16)
             for r in range(2)]
    return jnp.concatenate(parts, axis=1)


def _mxu_accumulate(acc, lhs_halves, tile, reg, transpose):
    def step(h):
        pltpu.matmul_push_rhs(_rhs_half(tile, h), staging_register=reg, mxu_index=h, transpose=transpose)
        pltpu.matmul_acc_lhs(acc, lhs_halves[h], mxu_index=h, load_staged_rhs=reg)

    return [functools.partial(step, h) for h in range(2)]


def _mxu_pop(acc):
    return [pltpu.matmul_pop(acc, (SLAB_ROWS, 256), _F32, mxu_index=h) for h in range(2)]


def _row_masks():
    sub = lax.broadcasted_iota(jnp.int32, (SLAB_ROWS, 2 * N_PAIRS), 0)
    return [((sub % GATHER_NT) == n).astype(_F32) for n in range(GATHER_NT)]


def _parity_mask():
    lane = lax.broadcasted_iota(jnp.int32, (SLAB_ROWS, 2 * N_PAIRS), 1)
    sub = lax.broadcasted_iota(jnp.int32, (SLAB_ROWS, 2 * N_PAIRS), 0)
    return ((sub // GATHER_NT) == (lane & 1)).astype(_F32)


def _pipelined_tokens(idx_ref, tbl_ref, tiles, accumulate, drain):
    tb = idx_ref.shape[0]
    n_stage = tb // GATHER_NT
    sets = (tiles[:GATHER_NT], tiles[GATHER_NT:])
    slab_tokens = lambda j: [GATHER_NT * j + n for n in range(GATHER_NT)]

    def stage(j, cur, nxt, acc):
        drain(jnp.maximum(j - 2, 0), acc)
        steps = []
        for n in range(GATHER_NT):
            steps += accumulate(j, n, cur[n], acc, n % 2)
        ahead = slab_tokens(jnp.minimum(j + 1, n_stage - 1))
        part = N_PAIRS // len(steps)
        for i, step in enumerate(steps):
            step()
            _gather_rows(idx_ref, ahead, tbl_ref, nxt, part * i, part * (i + 1))

    def body(j, carry):
        for p in range(2):
            @pl.when((j & 1) == p)
            def _():
                stage(j, sets[p], sets[1 - p], MXU_ACC[p])

        return carry

    @pl.when(pl.program_id(0) == 0)
    def _():
        for acc in MXU_ACC:
            _mxu_pop(acc)

    _gather_rows(idx_ref, slab_tokens(0), tbl_ref, sets[0])
    lax.fori_loop(0, n_stage, body, 0)
    for j in (n_stage - 2, n_stage - 1):
        drain(j, MXU_ACC[j % 2])


def _slab_rows(slab):
    return pl.ds(pl.multiple_of(GATHER_NT * slab, GATHER_NT), GATHER_NT)


def _u_pass_kernel(idx_ref, x_ref, gate_ref, tbl_ref, out_ref, *scratch):
    tiles, ybuf = scratch[:-1], scratch[-1]
    row_masks = _row_masks()
    parity = _parity_mask()

    def accumulate(slab, n, tile, acc, reg):
        x8 = x_ref[_slab_rows(slab), :]
        xs = jnp.concatenate([x8[:, :HALF_D], x8[:, HALF_D:]], axis=0)
        lhs = (xs * row_masks[n][:, :1]).astype(_BF16)
        return _mxu_accumulate(acc, [lhs[:, :256], lhs[:, 256:]], tile, reg, transpose=True)

    def drain(slab, acc):
        z0, z1 = _mxu_pop(acc)
        z = (z0 + z1) * parity
        ybuf[_slab_rows(slab), :] = z[:GATHER_NT] + z[GATHER_NT:]

    _pipelined_tokens(idx_ref, tbl_ref, tiles, accumulate, drain)
    y = ybuf[...]
    lane = lax.broadcasted_iota(jnp.int32, y.shape, 1)
    other = jnp.where((lane & 1) == 0, pltpu.roll(y, 2 * N_PAIRS - 1, axis=1), pltpu.roll(y, 1, axis=1))
    out_ref[...] = gate_ref[...] * _gelu(y + other)


def _v_pass_kernel(idx_ref, w_ref, tbl_ref, out_ref, *tiles):
    row_masks = _row_masks()
    parity = _parity_mask()

    def accumulate(slab, n, tile, acc, reg):
        w8 = w_ref[_slab_rows(slab), :]
        ws = jnp.concatenate([w8, w8], axis=0)
        lhs = (ws * (row_masks[n] * parity)).astype(_BF16)
        return _mxu_accumulate(acc, [lhs, lhs], tile, reg, transpose=False)

    def drain(slab, acc):
        for h, r in enumerate(_mxu_pop(acc)):
            for b in range(2):
                out_ref[_slab_rows(slab), pl.ds(HALF_D * b + 256 * h, 256)] = r[GATHER_NT * b:GATHER_NT * (b + 1), :]

    _pipelined_tokens(idx_ref, tbl_ref, tiles, accumulate, drain)


def _gather_call(kernel_fn, name, n_tok, in_specs, out_spec, out_shape, n_extra_scratch, args):
    tb = GATHER_TB
    tile = pltpu.VMEM((ROW_WORDS * N_PAIRS, 128), jnp.int32)
    return pl.pallas_call(
        kernel_fn,
        out_shape=out_shape,
        grid=(n_tok // tb,),
        in_specs=in_specs,
        out_specs=out_spec,
        scratch_shapes=[tile] * (2 * GATHER_NT) + n_extra_scratch,
        compiler_params=pltpu.CompilerParams(
            dimension_semantics=("arbitrary",),
            vmem_limit_bytes=GATHER_VMEM_BYTES,
        ),
        name=name,
    )(*args)


def _u_pass(idx, h2, gate2, tbl):
    n_tok = idx.shape[0]
    tb = GATHER_TB
    return _gather_call(
        _u_pass_kernel, "peer_u_pass", n_tok,
        [pl.BlockSpec((tb, N_PAIRS), lambda i: (i, 0), memory_space=pltpu.SMEM),
         pl.BlockSpec((tb, D_MODEL), lambda i: (i, 0)),
         pl.BlockSpec((tb, 2 * N_PAIRS), lambda i: (i, 0)),
         _resident()],
        pl.BlockSpec((tb, 2 * N_PAIRS), lambda i: (i, 0)),
        jax.ShapeDtypeStruct((n_tok, 2 * N_PAIRS), _F32),
        [pltpu.VMEM((tb, 2 * N_PAIRS), _F32)],
        (idx, h2, gate2, tbl))


def _v_pass(idx, w2, tbl):
    n_tok = idx.shape[0]
    tb = GATHER_TB
    return _gather_call(
        _v_pass_kernel, "peer_v_pass", n_tok,
        [pl.BlockSpec((tb, N_PAIRS), lambda i: (i, 0), memory_space=pltpu.SMEM),
         pl.BlockSpec((tb, 2 * N_PAIRS), lambda i: (i, 0)),
         _resident()],
        pl.BlockSpec((tb, D_MODEL), lambda i: (i, 0)),
        jax.ShapeDtypeStruct((n_tok, D_MODEL), _F32),
        [],
        (idx, w2, tbl))


def _final_kernel(x1_ref, ffn_ref, mod_ref, g_ref, b_ref, o_ref):
    g2 = mod_ref[0][5:6]
    o_ref[0] = _ln(DEEPNORM_ALPHA * x1_ref[0] + g2 * ffn_ref[0]) * g_ref[...] + b_ref[...]


def _final_norm(x1, ffn, mod3, g, b):
    bsz, seq, d = x1.shape
    ts = NORM_TS
    blk = pl.BlockSpec((1, ts, d), lambda i, s: (i, s, 0))
    vec = pl.BlockSpec((1, d), lambda i, s: (0, 0))
    return pl.pallas_call(
        _final_kernel,
        out_shape=jax.ShapeDtypeStruct((bsz, seq, d), _F32),
        grid=(bsz, seq // ts),
        in_specs=[blk, blk, pl.BlockSpec((1, 6, d), lambda i, s: (i, 0, 0)), vec, vec],
        out_specs=blk,
        compiler_params=pltpu.CompilerParams(dimension_semantics=("arbitrary", "arbitrary")),
        name="final_norm",
    )(x1, ffn, mod3, g, b)


def kernel(x, c, w_cond, b_cond, w_in, gmlp_ln_g, gmlp_ln_b, w_spatial, b_spatial, conv_w, p_a, p_b, w_o, ln1_g, ln1_b, w_q_peer, sub_keys, expert_u, expert_v, ln2_g, ln2_b):
    bsz, seq, d = x.shape
    n_tok = bsz * seq
    depth = w_cond.shape[0]
    row = lambda v: v.reshape(1, d)
    for l in range(depth):
        mod3 = _cond_proj(c, w_cond[l], b_cond[l]).reshape(bsz, 6, d)
        x1, h2 = _mixer(
            x, mod3, w_in[l].astype(_BF16), row(gmlp_ln_g[l]), row(gmlp_ln_b[l]),
            w_spatial[l].astype(_BF16), b_spatial[l][:, :, None], conv_w[l],
            p_a[l].astype(_BF16), p_b[l].astype(_BF16), w_o[l].astype(_BF16),
            row(ln1_g[l]), row(ln1_b[l]))
        h2 = h2.reshape(n_tok, d)
        keys = sub_keys[l].astype(_BF16).reshape(2 * PEER_HEADS, PEER_N_KEYS, PEER_HALF)
        idx, gate2 = _route(h2, w_q_peer[l].astype(_BF16), keys)
        w2 = _u_pass(idx, h2, gate2, _pack_table(expert_u[l]))
        ffn = _v_pass(idx, w2, _pack_table(expert_v[l])).reshape(bsz, seq, d)
        x = _final_norm(x1, ffn, mod3, row(ln2_g[l]), row(ln2_b[l]))
    return x
```

```python
import functools

import jax
import jax.numpy as jnp
from jax import lax
from jax.experimental import pallas as pl
from jax.experimental.pallas import tpu as pltpu

D_MODEL = 1024
CHUNK = 64
GMLP_BLOCK = 128
GMLP_GROUPS = 8
CONV_K = 3
PEER_HEADS = 8
PEER_HALF = 128
PEER_N_KEYS = 128
PEER_TOPK = 16
N_PAIRS = PEER_HEADS * PEER_TOPK
DEEPNORM_ALPHA = 2.0 ** 0.25
LN_EPS = 1e-5

ROW_WORDS = 4
HALF_D = D_MODEL // 2
MIX_TS = 512
ROUTE_TT = 256
GATHER_TB = 512
GATHER_NT = 8
SLAB_ROWS = 2 * GATHER_NT
NORM_TS = 512
PACK_TE = 512
MXU_ACC = (0, 4)
GATHER_VMEM_BYTES = 48 * 1024 * 1024
MIXER_VMEM_BYTES = 56 * 1024 * 1024

_F32 = jnp.float32
_BF16 = jnp.bfloat16
_NEG_INF = float("-inf")


def _ln(x):
    mu = jnp.mean(x, axis=-1, keepdims=True)
    xc = x - mu
    var = jnp.mean(xc * xc, axis=-1, keepdims=True)
    return xc * lax.rsqrt(var + LN_EPS)


def _gelu(x):
    return 0.5 * x * (1.0 + jnp.tanh(0.7978845608028654 * (x + 0.044715 * (x * x * x))))


def _sigmoid(x):
    return 1.0 / (1.0 + jnp.exp(-x))


def _resident():
    return pl.BlockSpec(memory_space=pltpu.VMEM)


def _cond_kernel(c_ref, w_ref, b_ref, o_ref):
    c = c_ref[...]
    a = (c * _sigmoid(c)).astype(_BF16)
    o_ref[...] = jnp.dot(a, w_ref[...].astype(_BF16), preferred_element_type=_F32) + b_ref[...]


def _cond_proj(c, w, b):
    bsz, d = c.shape
    n = w.shape[1]
    tn = 1024
    return pl.pallas_call(
        _cond_kernel,
        out_shape=jax.ShapeDtypeStruct((bsz, n), _F32),
        grid=(n // tn,),
        in_specs=[
            pl.BlockSpec((bsz, d), lambda j: (0, 0)),
            pl.BlockSpec((d, tn), lambda j: (0, j)),
            pl.BlockSpec((1, tn), lambda j: (0, j)),
        ],
        out_specs=pl.BlockSpec((bsz, tn), lambda j: (0, j)),
        compiler_params=pltpu.CompilerParams(dimension_semantics=("arbitrary",)),
        name="cond_proj",
    )(c, w, b.reshape(1, n))


def _mixer_kernel(x_ref, mod_ref, w_in_ref, gg_ref, gb_ref, ws_ref, bs_ref, cw_ref, pa_ref, pb_ref,
                  wo_ref, l1g_ref, l1b_ref, x1_ref, h2_ref, prev_ref, gu_ref, vn_ref, ya_ref):
    ts = x_ref.shape[1]
    d = D_MODEL

    @pl.when(pl.program_id(1) == 0)
    def _():
        prev_ref[...] = jnp.zeros_like(prev_ref)

    x = x_ref[0]
    mod = mod_ref[0]
    sh1, sc1, g1, sh2, sc2 = (mod[i:i + 1] for i in range(5))
    h = (_ln(x) * (1.0 + sc1) + sh1).astype(_BF16)

    def proj(j):
        return jnp.dot(h, w_in_ref[:, j * d:(j + 1) * d], preferred_element_type=_F32)

    gu_ref[...] = _gelu(proj(0))
    vn_ref[...] = (_ln(_gelu(proj(1))) * gg_ref[...] + gb_ref[...]).astype(_BF16)
    qi = lax.broadcasted_iota(jnp.int32, (GMLP_BLOCK, GMLP_BLOCK), 0) // CHUNK
    kj = lax.broadcasted_iota(jnp.int32, (GMLP_BLOCK, GMLP_BLOCK), 1) // CHUNK
    causal = kj <= qi
    for g in range(GMLP_GROUPS):
        wg = jnp.where(causal, ws_ref[g], jnp.zeros((), _BF16))
        cols = slice(g * GMLP_BLOCK, (g + 1) * GMLP_BLOCK)
        for n in range(ts // GMLP_BLOCK):
            rows = slice(n * GMLP_BLOCK, (n + 1) * GMLP_BLOCK)
            mixed = jnp.dot(wg, vn_ref[rows, cols], preferred_element_type=_F32) + bs_ref[g]
            ya_ref[rows, cols] = (gu_ref[rows, cols] * mixed).astype(_BF16)

    g_b = proj(2)
    zc = proj(3) * proj(4)
    ext = jnp.concatenate([prev_ref[...], zc], axis=0)
    prev_ref[...] = zc[ts - 8:, :]
    cw = cw_ref[...]
    y = ext[6:ts + 6, :] * cw[0:1] + ext[7:ts + 7, :] * cw[1:2] + zc * cw[2:3]
    y_b = (g_b * y).astype(_BF16)

    merged = (_sigmoid(proj(5)) * jnp.dot(ya_ref[...], pa_ref[...], preferred_element_type=_F32)
              + _sigmoid(proj(6)) * jnp.dot(y_b, pb_ref[...], preferred_element_type=_F32))
    mix = jnp.dot(merged.astype(_BF16), wo_ref[...], preferred_element_type=_F32)
    x1 = _ln(DEEPNORM_ALPHA * x + g1 * mix) * l1g_ref[...] + l1b_ref[...]
    x1_ref[0] = x1
    h2_ref[0] = _ln(x1) * (1.0 + sc2) + sh2


def _mixer(x, mod3, w_in, gg, gb, ws, bs, cw, pa, pb, wo, l1g, l1b):
    bsz, seq, d = x.shape
    ts = MIX_TS
    blk = pl.BlockSpec((1, ts, d), lambda b, s: (b, s, 0))
    return pl.pallas_call(
        _mixer_kernel,
        out_shape=(jax.ShapeDtypeStruct((bsz, seq, d), _F32),
                   jax.ShapeDtypeStruct((bsz, seq, d), _F32)),
        grid=(bsz, seq // ts),
        in_specs=[blk, pl.BlockSpec((1, 6, d), lambda b, s: (b, 0, 0))] + [_resident()] * 11,
        out_specs=(blk, blk),
        scratch_shapes=[
            pltpu.VMEM((8, d), _F32),
            pltpu.VMEM((ts, d), _F32),
            pltpu.VMEM((ts, d), _BF16),
            pltpu.VMEM((ts, d), _BF16),
        ],
        compiler_params=pltpu.CompilerParams(
            dimension_semantics=("arbitrary", "arbitrary"),
            vmem_limit_bytes=MIXER_VMEM_BYTES,
        ),
        name="mixer",
    )(x, mod3, w_in, gg, gb, ws, bs, cw, pa, pb, wo, l1g, l1b)


def _sorted_top(vals, ids, payload, k):
    vals, ids = list(vals), list(ids)
    pay = None if payload is None else list(payload)
    g = len(vals)
    for phase in range(g):
        for i in range(phase % 2, g - 1, 2):
            swap = vals[i + 1] > vals[i]
            vals[i], vals[i + 1] = jnp.maximum(vals[i], vals[i + 1]), jnp.minimum(vals[i], vals[i + 1])
            ids[i], ids[i + 1] = jnp.where(swap, ids[i + 1], ids[i]), jnp.where(swap, ids[i], ids[i + 1])
            if pay is not None:
                pay[i], pay[i + 1] = jnp.where(swap, pay[i + 1], pay[i]), jnp.where(swap, pay[i], pay[i + 1])
    big = jnp.int32(2 ** 30)
    top_v, top_p = [], []
    for r in range(k):
        m = jnp.max(vals[0], axis=0, keepdims=True)
        sel = jnp.min(jnp.where(vals[0] == m, ids[0], big), axis=0, keepdims=True)
        hit = ids[0] == sel
        top_v.append(m)
        top_p.append(sel if pay is None else jnp.max(jnp.where(hit, pay[0], -1), axis=0, keepdims=True))
        for i in range(min(g, k - r - 1)):
            nxt = i + 1 < g
            vals[i] = jnp.where(hit, vals[i + 1] if nxt else _NEG_INF, vals[i])
            ids[i] = jnp.where(hit, ids[i + 1] if nxt else big, ids[i])
            if pay is not None and nxt:
                pay[i] = jnp.where(hit, pay[i + 1], pay[i])
    return top_v, top_p


def _route_kernel(h_ref, wq_ref, keys_ref, idx_ref, gate_ref, st_ref, it_ref, idx_t_ref, gate_t_ref):
    tt = h_ref.shape[0]
    lanes = 128
    q = jnp.dot(h_ref[...].astype(_BF16), wq_ref[...], preferred_element_type=_F32).astype(_BF16)
    row = lax.broadcasted_iota(jnp.int32, (8, lanes), 0)
    groups = PEER_N_KEYS // 8
    for hp in range(2 * PEER_HEADS):
        s = lax.dot_general(keys_ref[hp], q[:, hp * PEER_HALF:(hp + 1) * PEER_HALF],
                            (((1,), (1,)), ((), ())), preferred_element_type=_F32)
        for c in range(tt // lanes):
            cols = slice(c * lanes, (c + 1) * lanes)
            top_v, top_i = _sorted_top([s[8 * i:8 * i + 8, cols] for i in range(groups)],
                                       [row + 8 * i for i in range(groups)], None, PEER_TOPK)
            for r in range(PEER_TOPK):
                st_ref[hp, r:r + 1, cols] = top_v[r]
                it_ref[hp, r:r + 1, cols] = top_i[r]

    for hd in range(PEER_HEADS):
        for c in range(tt // lanes):
            cols = slice(c * lanes, (c + 1) * lanes)
            s1, s2 = st_ref[2 * hd, :, cols], st_ref[2 * hd + 1, :, cols]
            i1, i2 = it_ref[2 * hd, :, cols], it_ref[2 * hd + 1, :, cols]
            vals, eids, cids = [], [], []
            for a, b0 in ((0, 0), (0, 8), (1, 0), (2, 0)):
                vals.append(s1[a:a + 1] + s2[b0:b0 + 8])
                eids.append(i1[a:a + 1] * PEER_N_KEYS + i2[b0:b0 + 8])
                cids.append(a * PEER_TOPK + b0 + row)
            t2, u2 = s2[0:8], i2[0:8]
            lo4 = row < 4
            vals.append(jnp.where(row == 7, _NEG_INF, jnp.where(lo4, s1[3:4], s1[4:5])
                                  + jnp.where(lo4, t2, pltpu.roll(t2, 4, axis=0))))
            eids.append(jnp.where(lo4, i1[3:4], i1[4:5]) * PEER_N_KEYS
                        + jnp.where(lo4, u2, pltpu.roll(u2, 4, axis=0)))
            cids.append(jnp.where(lo4, 3 * PEER_TOPK + row, 4 * PEER_TOPK - 4 + row))
            lo2 = row < 2
            vals.append(jnp.where(row >= 6, _NEG_INF,
                                  jnp.where(lo2, s1[5:6], jnp.where(lo4, s1[6:7], s1[7:8]))
                                  + jnp.where(lo2, t2, jnp.where(lo4, pltpu.roll(t2, 2, axis=0),
                                                                 pltpu.roll(t2, 4, axis=0)))))
            eids.append(jnp.where(lo2, i1[5:6], jnp.where(lo4, i1[6:7], i1[7:8])) * PEER_N_KEYS
                        + jnp.where(lo2, u2, jnp.where(lo4, pltpu.roll(u2, 2, axis=0),
                                                       pltpu.roll(u2, 4, axis=0))))
            cids.append(jnp.where(lo2, 5 * PEER_TOPK + row,
                                  jnp.where(lo4, 6 * PEER_TOPK - 2 + row, 7 * PEER_TOPK - 4 + row)))
            vals.append(s1[8:16] + s2[0:1])
            eids.append(i1[8:16] * PEER_N_KEYS + i2[0:1])
            cids.append((8 + row) * PEER_TOPK)
            top_s, top_e = _sorted_top(vals, cids, eids, PEER_TOPK)
            ex = [jnp.exp(v - top_s[0]) for v in top_s]
            denom = functools.reduce(lambda p, r: p + r, ex)
            for r in range(PEER_TOPK):
                k = hd * PEER_TOPK + r
                idx_t_ref[k:k + 1, cols] = top_e[r] * ROW_WORDS
                gate_t_ref[2 * k:2 * k + 2, cols] = jnp.broadcast_to(ex[r] / denom, (2, lanes))
    idx_ref[...] = idx_t_ref[...].T
    gate_ref[...] = gate_t_ref[...].T


def _route(h2, wq, keys):
    n_tok, d = h2.shape
    tt = ROUTE_TT
    return pl.pallas_call(
        _route_kernel,
        out_shape=(jax.ShapeDtypeStruct((n_tok, N_PAIRS), jnp.int32),
                   jax.ShapeDtypeStruct((n_tok, 2 * N_PAIRS), _F32)),
        grid=(n_tok // tt,),
        in_specs=[pl.BlockSpec((tt, d), lambda i: (i, 0)), _resident(), _resident()],
        out_specs=(pl.BlockSpec((tt, N_PAIRS), lambda i: (i, 0)),
                   pl.BlockSpec((tt, 2 * N_PAIRS), lambda i: (i, 0))),
        scratch_shapes=[
            pltpu.VMEM((2 * PEER_HEADS, PEER_TOPK, tt), _F32),
            pltpu.VMEM((2 * PEER_HEADS, PEER_TOPK, tt), jnp.int32),
            pltpu.VMEM((N_PAIRS, tt), jnp.int32),
            pltpu.VMEM((2 * N_PAIRS, tt), _F32),
        ],
        compiler_params=pltpu.CompilerParams(dimension_semantics=("arbitrary",)),
        name="peer_route",
    )(h2, wq, keys)


def _pack_kernel(w_ref, o_ref):
    n = w_ref.shape[0]
    w = w_ref[...]
    lo = pltpu.bitcast(w[:, :HALF_D].astype(_BF16).astype(_F32), jnp.uint32) >> 16
    hi = pltpu.bitcast(w[:, HALF_D:].astype(_BF16).astype(_F32), jnp.uint32) & jnp.uint32(0xFFFF0000)
    words = pltpu.bitcast(lo | hi, jnp.int32)
    for r in range(ROW_WORDS):
        o_ref[pl.ds(r, n, stride=ROW_WORDS), :] = words[:, 128 * r:128 * (r + 1)]


def _pack_table(w):
    n, d = w.shape
    te = PACK_TE
    return pl.pallas_call(
        _pack_kernel,
        out_shape=jax.ShapeDtypeStruct((n * ROW_WORDS, 128), jnp.int32),
        grid=(n // te,),
        in_specs=[pl.BlockSpec((te, d), lambda i: (i, 0))],
        out_specs=pl.BlockSpec((te * ROW_WORDS, 128), lambda i: (i, 0)),
        compiler_params=pltpu.CompilerParams(dimension_semantics=("arbitrary",)),
        name="pack_table",
    )(w)


def _gather_rows(idx_ref, tokens, tbl_ref, slab_tiles, lo=0, hi=N_PAIRS):
    rows = [idx_ref.at[t] for t in tokens]
    for k in range(lo, hi):
        for row, tile in zip(rows, slab_tiles):
            i = pl.multiple_of(row[k], ROW_WORDS)
            tile[pl.ds(ROW_WORDS * k, ROW_WORDS), :] = tbl_ref[pl.ds(i, ROW_WORDS), :]


def _rhs_half(tile, h):
    parts = [pltpu.bitcast(tile[pl.ds(2 * h + r, N_PAIRS, stride=ROW_WORDS), :], _BF16)
             for r in range(2)]
    return jnp.concatenate(parts, axis=1)


def _mxu_accumulate(acc, lhs_halves, tile, reg, transpose):
    def step(h):
        pltpu.matmul_push_rhs(_rhs_half(tile, h), staging_register=reg, mxu_index=h, transpose=transpose)
        pltpu.matmul_acc_lhs(acc, lhs_halves[h], mxu_index=h, load_staged_rhs=reg)

    return [functools.partial(step, h) for h in range(2)]


def _mxu_pop(acc):
    return [pltpu.matmul_pop(acc, (SLAB_ROWS, 256), _F32, mxu_index=h) for h in range(2)]


def _row_masks():
    sub = lax.broadcasted_iota(jnp.int32, (SLAB_ROWS, 2 * N_PAIRS), 0)
    return [((sub % GATHER_NT) == n).astype(_F32) for n in range(GATHER_NT)]


def _parity_mask():
    lane = lax.broadcasted_iota(jnp.int32, (SLAB_ROWS, 2 * N_PAIRS), 1)
    sub = lax.broadcasted_iota(jnp.int32, (SLAB_ROWS, 2 * N_PAIRS), 0)
    return ((sub // GATHER_NT) == (lane & 1)).astype(_F32)


def _pipelined_tokens(idx_ref, tbl_ref, tiles, accumulate, drain):
    tb = idx_ref.shape[0]
    n_stage = tb // GATHER_NT
    sets = (tiles[:GATHER_NT], tiles[GATHER_NT:])
    slab_tokens = lambda j: [GATHER_NT * j + n for n in range(GATHER_NT)]

    def stage(j, cur, nxt, acc):
        drain(jnp.maximum(j - 2, 0), acc)
        steps = []
        for n in range(GATHER_NT):
            steps += accumulate(j, n, cur[n], acc, n % 2)
        ahead = slab_tokens(jnp.minimum(j + 1, n_stage - 1))
        part = N_PAIRS // len(steps)
        for i, step in enumerate(steps):
            step()
            _gather_rows(idx_ref, ahead, tbl_ref, nxt, part * i, part * (i + 1))

    def body(j, carry):
        for p in range(2):
            @pl.when((j & 1) == p)
            def _():
                stage(j, sets[p], sets[1 - p], MXU_ACC[p])

        return carry

    @pl.when(pl.program_id(0) == 0)
    def _():
        for acc in MXU_ACC:
            _mxu_pop(acc)

    _gather_rows(idx_ref, slab_tokens(0), tbl_ref, sets[0])
    lax.fori_loop(0, n_stage, body, 0)
    for j in (n_stage - 2, n_stage - 1):
        drain(j, MXU_ACC[j % 2])


def _slab_rows(slab):
    return pl.ds(pl.multiple_of(GATHER_NT * slab, GATHER_NT), GATHER_NT)


def _u_pass_kernel(idx_ref, x_ref, gate_ref, tbl_ref, out_ref, *scratch):
    tiles, ybuf = scratch[:-1], scratch[-1]
    row_masks = _row_masks()
    parity = _parity_mask()

    def accumulate(slab, n, tile, acc, reg):
        x8 = x_ref[_slab_rows(slab), :]
        xs = jnp.concatenate([x8[:, :HALF_D], x8[:, HALF_D:]], axis=0)
        lhs = (xs * row_masks[n][:, :1]).astype(_BF16)
        return _mxu_accumulate(acc, [lhs[:, :256], lhs[:, 256:]], tile, reg, transpose=True)

    def drain(slab, acc):
        z0, z1 = _mxu_pop(acc)
        z = (z0 + z1) * parity
        ybuf[_slab_rows(slab), :] = z[:GATHER_NT] + z[GATHER_NT:]

    _pipelined_tokens(idx_ref, tbl_ref, tiles, accumulate, drain)
    y = ybuf[...]
    lane = lax.broadcasted_iota(jnp.int32, y.shape, 1)
    other = jnp.where((lane & 1) == 0, pltpu.roll(y, 2 * N_PAIRS - 1, axis=1), pltpu.roll(y, 1, axis=1))
    out_ref[...] = gate_ref[...] * _gelu(y + other)


def _v_pass_kernel(idx_ref, w_ref, tbl_ref, out_ref, *tiles):
    row_masks = _row_masks()
    parity = _parity_mask()

    def accumulate(slab, n, tile, acc, reg):
        w8 = w_ref[_slab_rows(slab), :]
        ws = jnp.concatenate([w8, w8], axis=0)
        lhs = (ws * (row_masks[n] * parity)).astype(_BF16)
        return _mxu_accumulate(acc, [lhs, lhs], tile, reg, transpose=False)

    def drain(slab, acc):
        for h, r in enumerate(_mxu_pop(acc)):
            for b in range(2):
                out_ref[_slab_rows(slab), pl.ds(HALF_D * b + 256 * h, 256)] = r[GATHER_NT * b:GATHER_NT * (b + 1), :]

    _pipelined_tokens(idx_ref, tbl_ref, tiles, accumulate, drain)


def _gather_call(kernel_fn, name, n_tok, in_specs, out_spec, out_shape, n_extra_scratch, args):
    tb = GATHER_TB
    tile = pltpu.VMEM((ROW_WORDS * N_PAIRS, 128), jnp.int32)
    return pl.pallas_call(
        kernel_fn,
        out_shape=out_shape,
        grid=(n_tok // tb,),
        in_specs=in_specs,
        out_specs=out_spec,
        scratch_shapes=[tile] * (2 * GATHER_NT) + n_extra_scratch,
        compiler_params=pltpu.CompilerParams(
            dimension_semantics=("arbitrary",),
            vmem_limit_bytes=GATHER_VMEM_BYTES,
        ),
        name=name,
    )(*args)


def _u_pass(idx, h2, gate2, tbl):
    n_tok = idx.shape[0]
    tb = GATHER_TB
    return _gather_call(
        _u_pass_kernel, "peer_u_pass", n_tok,
        [pl.BlockSpec((tb, N_PAIRS), lambda i: (i, 0), memory_space=pltpu.SMEM),
         pl.BlockSpec((tb, D_MODEL), lambda i: (i, 0)),
         pl.BlockSpec((tb, 2 * N_PAIRS), lambda i: (i, 0)),
         _resident()],
        pl.BlockSpec((tb, 2 * N_PAIRS), lambda i: (i, 0)),
        jax.ShapeDtypeStruct((n_tok, 2 * N_PAIRS), _F32),
        [pltpu.VMEM((tb, 2 * N_PAIRS), _F32)],
        (idx, h2, gate2, tbl))


def _v_pass(idx, w2, tbl):
    n_tok = idx.shape[0]
    tb = GATHER_TB
    return _gather_call(
        _v_pass_kernel, "peer_v_pass", n_tok,
        [pl.BlockSpec((tb, N_PAIRS), lambda i: (i, 0), memory_space=pltpu.SMEM),
         pl.BlockSpec((tb, 2 * N_PAIRS), lambda i: (i, 0)),
         _resident()],
        pl.BlockSpec((tb, D_MODEL), lambda i: (i, 0)),
        jax.ShapeDtypeStruct((n_tok, D_MODEL), _F32),
        [],
        (idx, w2, tbl))


def _final_kernel(x1_ref, ffn_ref, mod_ref, g_ref, b_ref, o_ref):
    g2 = mod_ref[0][5:6]
    o_ref[0] = _ln(DEEPNORM_ALPHA * x1_ref[0] + g2 * ffn_ref[0]) * g_ref[...] + b_ref[...]


def _final_norm(x1, ffn, mod3, g, b):
    bsz, seq, d = x1.shape
    ts = NORM_TS
    blk = pl.BlockSpec((1, ts, d), lambda i, s: (i, s, 0))
    vec = pl.BlockSpec((1, d), lambda i, s: (0, 0))
    return pl.pallas_call(
        _final_kernel,
        out_shape=jax.ShapeDtypeStruct((bsz, seq, d), _F32),
        grid=(bsz, seq // ts),
        in_specs=[blk, blk, pl.BlockSpec((1, 6, d), lambda i, s: (i, 0, 0)), vec, vec],
        out_specs=blk,
        compiler_params=pltpu.CompilerParams(dimension_semantics=("arbitrary", "arbitrary")),
        name="final_norm",
    )(x1, ffn, mod3, g, b)


def kernel(x, c, w_cond, b_cond, w_in, gmlp_ln_g, gmlp_ln_b, w_spatial, b_spatial, conv_w, p_a, p_b, w_o, ln1_g, ln1_b, w_q_peer, sub_keys, expert_u, expert_v, ln2_g, ln2_b):
    bsz, seq, d = x.shape
    n_tok = bsz * seq
    depth = w_cond.shape[0]
    row = lambda v: v.reshape(1, d)
    for l in range(depth):
        mod3 = _cond_proj(c, w_cond[l], b_cond[l]).reshape(bsz, 6, d)
        x1, h2 = _mixer(
            x, mod3, w_in[l].astype(_BF16), row(gmlp_ln_g[l]), row(gmlp_ln_b[l]),
            w_spatial[l].astype(_BF16), b_spatial[l][:, :, None], conv_w[l],
            p_a[l].astype(_BF16), p_b[l].astype(_BF16), w_o[l].astype(_BF16),
            row(ln1_g[l]), row(ln1_b[l]))
        h2 = h2.reshape(n_tok, d)
        keys = sub_keys[l].astype(_BF16).reshape(2 * PEER_HEADS, PEER_N_KEYS, PEER_HALF)
        idx, gate2 = _route(h2, w_q_peer[l].astype(_BF16), keys)
        w2 = _u_pass(idx, h2, gate2, _pack_table(expert_u[l]))
        ffn = _v_pass(idx, w2, _pack_table(expert_v[l])).reshape(bsz, seq, d)
        x = _final_norm(x1, ffn, mod3, row(ln2_g[l]), row(ln2_b[l]))
    return x
```

```python
import functools

import jax
import jax.numpy as jnp
from jax import lax
from jax.experimental import pallas as pl
from jax.experimental.pallas import tpu as pltpu

D_MODEL = 1024
CHUNK = 64
GMLP_BLOCK = 128
GMLP_GROUPS = 8
CONV_K = 3
PEER_HEADS = 8
PEER_HALF = 128
PEER_N_KEYS = 128
PEER_TOPK = 16
N_PAIRS = PEER_HEADS * PEER_TOPK
DEEPNORM_ALPHA = 2.0 ** 0.25
LN_EPS = 1e-5

ROW_WORDS = 4
HALF_D = D_MODEL // 2
MIX_TS = 512
ROUTE_TT = 256
GATHER_TB = 512
GATHER_NT = 8
SLAB_ROWS = 2 * GATHER_NT
NORM_TS = 512
PACK_TE = 512
MXU_ACC = (0, 4)
GATHER_VMEM_BYTES = 48 * 1024 * 1024
MIXER_VMEM_BYTES = 56 * 1024 * 1024

_F32 = jnp.float32
_BF16 = jnp.bfloat16
_NEG_INF = float("-inf")


def _ln(x):
    mu = jnp.mean(x, axis=-1, keepdims=True)
    xc = x - mu
    var = jnp.mean(xc * xc, axis=-1, keepdims=True)
    return xc * lax.rsqrt(var + LN_EPS)


def _gelu(x):
    return 0.5 * x * (1.0 + jnp.tanh(0.7978845608028654 * (x + 0.044715 * (x * x * x))))


def _sigmoid(x):
    return 1.0 / (1.0 + jnp.exp(-x))


def _resident():
    return pl.BlockSpec(memory_space=pltpu.VMEM)


def _cond_kernel(c_ref, w_ref, b_ref, o_ref):
    c = c_ref[...]
    a = (c * _sigmoid(c)).astype(_BF16)
    o_ref[...] = jnp.dot(a, w_ref[...].astype(_BF16), preferred_element_type=_F32) + b_ref[...]


def _cond_proj(c, w, b):
    bsz, d = c.shape
    n = w.shape[1]
    tn = 1024
    return pl.pallas_call(
        _cond_kernel,
        out_shape=jax.ShapeDtypeStruct((bsz, n), _F32),
        grid=(n // tn,),
        in_specs=[
            pl.BlockSpec((bsz, d), lambda j: (0, 0)),
            pl.BlockSpec((d, tn), lambda j: (0, j)),
            pl.BlockSpec((1, tn), lambda j: (0, j)),
        ],
        out_specs=pl.BlockSpec((bsz, tn), lambda j: (0, j)),
        compiler_params=pltpu.CompilerParams(dimension_semantics=("arbitrary",)),
        name="cond_proj",
    )(c, w, b.reshape(1, n))


def _mixer_kernel(x_ref, mod_ref, w_in_ref, gg_ref, gb_ref, ws_ref, bs_ref, cw_ref, pa_ref, pb_ref,
                  wo_ref, l1g_ref, l1b_ref, x1_ref, h2_ref, prev_ref, gu_ref, vn_ref, ya_ref):
    ts = x_ref.shape[1]
    d = D_MODEL

    @pl.when(pl.program_id(1) == 0)
    def _():
        prev_ref[...] = jnp.zeros_like(prev_ref)

    x = x_ref[0]
    mod = mod_ref[0]
    sh1, sc1, g1, sh2, sc2 = (mod[i:i + 1] for i in range(5))
    h = (_ln(x) * (1.0 + sc1) + sh1).astype(_BF16)

    def proj(j):
        return jnp.dot(h, w_in_ref[:, j * d:(j + 1) * d], preferred_element_type=_F32)

    gu_ref[...] = _gelu(proj(0))
    vn_ref[...] = (_ln(_gelu(proj(1))) * gg_ref[...] + gb_ref[...]).astype(_BF16)
    qi = lax.broadcasted_iota(jnp.int32, (GMLP_BLOCK, GMLP_BLOCK), 0) // CHUNK
    kj = lax.broadcasted_iota(jnp.int32, (GMLP_BLOCK, GMLP_BLOCK), 1) // CHUNK
    causal = kj <= qi
    for g in range(GMLP_GROUPS):
        wg = jnp.where(causal, ws_ref[g], jnp.zeros((), _BF16))
        cols = slice(g * GMLP_BLOCK, (g + 1) * GMLP_BLOCK)
        for n in range(ts // GMLP_BLOCK):
            rows = slice(n * GMLP_BLOCK, (n + 1) * GMLP_BLOCK)
            mixed = jnp.dot(wg, vn_ref[rows, cols], preferred_element_type=_F32) + bs_ref[g]
            ya_ref[rows, cols] = (gu_ref[rows, cols] * mixed).astype(_BF16)

    g_b = proj(2)
    zc = proj(3) * proj(4)
    ext = jnp.concatenate([prev_ref[...], zc], axis=0)
    prev_ref[...] = zc[ts - 8:, :]
    cw = cw_ref[...]
    y = ext[6:ts + 6, :] * cw[0:1] + ext[7:ts + 7, :] * cw[1:2] + zc * cw[2:3]
    y_b = (g_b * y).astype(_BF16)

    merged = (_sigmoid(proj(5)) * jnp.dot(ya_ref[...], pa_ref[...], preferred_element_type=_F32)
              + _sigmoid(proj(6)) * jnp.dot(y_b, pb_ref[...], preferred_element_type=_F32))
    mix = jnp.dot(merged.astype(_BF16), wo_ref[...], preferred_element_type=_F32)
    x1 = _ln(DEEPNORM_ALPHA * x + g1 * mix) * l1g_ref[...] + l1b_ref[...]
    x1_ref[0] = x1
    h2_ref[0] = _ln(x1) * (1.0 + sc2) + sh2


def _mixer(x, mod3, w_in, gg, gb, ws, bs, cw, pa, pb, wo, l1g, l1b):
    bsz, seq, d = x.shape
    ts = MIX_TS
    blk = pl.BlockSpec((1, ts, d), lambda b, s: (b, s, 0))
    return pl.pallas_call(
        _mixer_kernel,
        out_shape=(jax.ShapeDtypeStruct((bsz, seq, d), _F32),
                   jax.ShapeDtypeStruct((bsz, seq, d), _F32)),
        grid=(bsz, seq // ts),
        in_specs=[blk, pl.BlockSpec((1, 6, d), lambda b, s: (b, 0, 0))] + [_resident()] * 11,
        out_specs=(blk, blk),
        scratch_shapes=[
            pltpu.VMEM((8, d), _F32),
            pltpu.VMEM((ts, d), _F32),
            pltpu.VMEM((ts, d), _BF16),
            pltpu.VMEM((ts, d), _BF16),
        ],
        compiler_params=pltpu.CompilerParams(
            dimension_semantics=("arbitrary", "arbitrary"),
            vmem_limit_bytes=MIXER_VMEM_BYTES,
        ),
        name="mixer",
    )(x, mod3, w_in, gg, gb, ws, bs, cw, pa, pb, wo, l1g, l1b)


def _sorted_top(vals, ids, payload, k):
    vals, ids = list(vals), list(ids)
    pay = None if payload is None else list(payload)
    g = len(vals)
    for phase in range(g):
        for i in range(phase % 2, g - 1, 2):
            swap = vals[i + 1] > vals[i]
            vals[i], vals[i + 1] = jnp.maximum(vals[i], vals[i + 1]), jnp.minimum(vals[i], vals[i + 1])
            ids[i], ids[i + 1] = jnp.where(swap, ids[i + 1], ids[i]), jnp.where(swap, ids[i], ids[i + 1])
            if pay is not None:
                pay[i], pay[i + 1] = jnp.where(swap, pay[i + 1], pay[i]), jnp.where(swap, pay[i], pay[i + 1])
    big = jnp.int32(2 ** 30)
    top_v, top_p = [], []
    for r in range(k):
        m = jnp.max(vals[0], axis=0, keepdims=True)
        sel = jnp.min(jnp.where(vals[0] == m, ids[0], big), axis=0, keepdims=True)
        hit = ids[0] == sel
        top_v.append(m)
        top_p.append(sel if pay is None else jnp.max(jnp.where(hit, pay[0], -1), axis=0, keepdims=True))
        for i in range(min(g, k - r - 1)):
            nxt = i + 1 < g
            vals[i] = jnp.where(hit, vals[i + 1] if nxt else _NEG_INF, vals[i])
            ids[i] = jnp.where(hit, ids[i + 1] if nxt else big, ids[i])
            if pay is not None and nxt:
                pay[i] = jnp.where(hit, pay[i + 1], pay[i])
    return top_v, top_p


def _route_kernel(h_ref, wq_ref, keys_ref, idx_ref, gate_ref, st_ref, it_ref, idx_t_ref, gate_t_ref):
    tt = h_ref.shape[0]
    lanes = 128
    q = jnp.dot(h_ref[...].astype(_BF16), wq_ref[...], preferred_element_type=_F32).astype(_BF16)
    row = lax.broadcasted_iota(jnp.int32, (8, lanes), 0)
    groups = PEER_N_KEYS // 8
    for hp in range(2 * PEER_HEADS):
        s = lax.dot_general(keys_ref[hp], q[:, hp * PEER_HALF:(hp + 1) * PEER_HALF],
                            (((1,), (1,)), ((), ())), preferred_element_type=_F32)
        for c in range(tt // lanes):
            cols = slice(c * lanes, (c + 1) * lanes)
            top_v, top_i = _sorted_top([s[8 * i:8 * i + 8, cols] for i in range(groups)],
                                       [row + 8 * i for i in range(groups)], None, PEER_TOPK)
            for r in range(PEER_TOPK):
                st_ref[hp, r:r + 1, cols] = top_v[r]
                it_ref[hp, r:r + 1, cols] = top_i[r]

    for hd in range(PEER_HEADS):
        for c in range(tt // lanes):
            cols = slice(c * lanes, (c + 1) * lanes)
            s1, s2 = st_ref[2 * hd, :, cols], st_ref[2 * hd + 1, :, cols]
            i1, i2 = it_ref[2 * hd, :, cols], it_ref[2 * hd + 1, :, cols]
            vals, eids, cids = [], [], []
            for a, b0 in ((0, 0), (0, 8), (1, 0), (2, 0)):
                vals.append(s1[a:a + 1] + s2[b0:b0 + 8])
                eids.append(i1[a:a + 1] * PEER_N_KEYS + i2[b0:b0 + 8])
                cids.append(a * PEER_TOPK + b0 + row)
            t2, u2 = s2[0:8], i2[0:8]
            lo4 = row < 4
            vals.append(jnp.where(row == 7, _NEG_INF, jnp.where(lo4, s1[3:4], s1[4:5])
                                  + jnp.where(lo4, t2, pltpu.roll(t2, 4, axis=0))))
            eids.append(jnp.where(lo4, i1[3:4], i1[4:5]) * PEER_N_KEYS
                        + jnp.where(lo4, u2, pltpu.roll(u2, 4, axis=0)))
            cids.append(jnp.where(lo4, 3 * PEER_TOPK + row, 4 * PEER_TOPK - 4 + row))
            lo2 = row < 2
            vals.append(jnp.where(row >= 6, _NEG_INF,
                                  jnp.where(lo2, s1[5:6], jnp.where(lo4, s1[6:7], s1[7:8]))
                                  + jnp.where(lo2, t2, jnp.where(lo4, pltpu.roll(t2, 2, axis=0),
                                                                 pltpu.roll(t2, 4, axis=0)))))
            eids.append(jnp.where(lo2, i1[5:6], jnp.where(lo4, i1[6:7], i1[7:8])) * PEER_N_KEYS
                        + jnp.where(lo2, u2, jnp.where(lo4, pltpu.roll(u2, 2, axis=0),
                                                       pltpu.roll(u2, 4, axis=0))))
            cids.append(jnp.where(lo2, 5 * PEER_TOPK + row,
                                  jnp.where(lo4, 6 * PEER_TOPK - 2 + row, 7 * PEER_TOPK - 4 + row)))
            vals.append(s1[8:16] + s2[0:1])
            eids.append(i1[8:16] * PEER_N_KEYS + i2[0:1])
            cids.append((8 + row) * PEER_TOPK)
            top_s, top_e = _sorted_top(vals, cids, eids, PEER_TOPK)
            ex = [jnp.exp(v - top_s[0]) for v in top_s]
            denom = functools.reduce(lambda p, r: p + r, ex)
            for r in range(PEER_TOPK):
                k = hd * PEER_TOPK + r
                idx_t_ref[k:k + 1, cols] = top_e[r] * ROW_WORDS
                gate_t_ref[2 * k:2 * k + 2, cols] = jnp.broadcast_to(ex[r] / denom, (2, lanes))
    idx_ref[...] = idx_t_ref[...].T
    gate_ref[...] = gate_t_ref[...].T


def _route(h2, wq, keys):
    n_tok, d = h2.shape
    tt = ROUTE_TT
    return pl.pallas_call(
        _route_kernel,
        out_shape=(jax.ShapeDtypeStruct((n_tok, N_PAIRS), jnp.int32),
                   jax.ShapeDtypeStruct((n_tok, 2 * N_PAIRS), _F32)),
        grid=(n_tok // tt,),
        in_specs=[pl.BlockSpec((tt, d), lambda i: (i, 0)), _resident(), _resident()],
        out_specs=(pl.BlockSpec((tt, N_PAIRS), lambda i: (i, 0)),
                   pl.BlockSpec((tt, 2 * N_PAIRS), lambda i: (i, 0))),
        scratch_shapes=[
            pltpu.VMEM((2 * PEER_HEADS, PEER_TOPK, tt), _F32),
            pltpu.VMEM((2 * PEER_HEADS, PEER_TOPK, tt), jnp.int32),
            pltpu.VMEM((N_PAIRS, tt), jnp.int32),
            pltpu.VMEM((2 * N_PAIRS, tt), _F32),
        ],
        compiler_params=pltpu.CompilerParams(dimension_semantics=("arbitrary",)),
        name="peer_route",
    )(h2, wq, keys)


def _pack_kernel(w_ref, o_ref):
    n = w_ref.shape[0]
    w = w_ref[...]
    lo = pltpu.bitcast(w[:, :HALF_D].astype(_BF16).astype(_F32), jnp.uint32) >> 16
    hi = pltpu.bitcast(w[:, HALF_D:].astype(_BF16).astype(_F32), jnp.uint32) & jnp.uint32(0xFFFF0000)
    words = pltpu.bitcast(lo | hi, jnp.int32)
    for r in range(ROW_WORDS):
        o_ref[pl.ds(r, n, stride=ROW_WORDS), :] = words[:, 128 * r:128 * (r + 1)]


def _pack_table(w):
    n, d = w.shape
    te = PACK_TE
    return pl.pallas_call(
        _pack_kernel,
        out_shape=jax.ShapeDtypeStruct((n * ROW_WORDS, 128), jnp.int32),
        grid=(n // te,),
        in_specs=[pl.BlockSpec((te, d), lambda i: (i, 0))],
        out_specs=pl.BlockSpec((te * ROW_WORDS, 128), lambda i: (i, 0)),
        compiler_params=pltpu.CompilerParams(dimension_semantics=("arbitrary",)),
        name="pack_table",
    )(w)


def _gather_rows(idx_ref, tokens, tbl_ref, slab_tiles, lo=0, hi=N_PAIRS):
    rows = [idx_ref.at[t] for t in tokens]
    for k in range(lo, hi):
        for row, tile in zip(rows, slab_tiles):
            i = pl.multiple_of(row[k], ROW_WORDS)
            tile[pl.ds(ROW_WORDS * k, ROW_WORDS), :] = tbl_ref[pl.ds(i, ROW_WORDS), :]


def _rhs_half(tile, h):
    parts = [pltpu.bitcast(tile[pl.ds(2 * h + r, N_PAIRS, stride=ROW_WORDS), :], _BF16)
             for r in range(2)]
    return jnp.concatenate(parts, axis=1)


def _mxu_accumulate(acc, lhs_halves, tile, reg, transpose):
    def step(h):
        pltpu.matmul_push_rhs(_rhs_half(tile, h), staging_register=reg, mxu_index=h, transpose=transpose)
        pltpu.matmul_acc_lhs(acc, lhs_halves[h], mxu_index=h, load_staged_rhs=reg)

    return [functools.partial(step, h) for h in range(2)]


def _mxu_pop(acc):
    return [pltpu.matmul_pop(acc, (SLAB_ROWS, 256), _F32, mxu_index=h) for h in range(2)]


def _row_masks():
    sub = lax.broadcasted_iota(jnp.int32, (SLAB_ROWS, 2 * N_PAIRS), 0)
    return [((sub % GATHER_NT) == n).astype(_F32) for n in range(GATHER_NT)]


def _parity_mask():
    lane = lax.broadcasted_iota(jnp.int32, (SLAB_ROWS, 2 * N_PAIRS), 1)
    sub = lax.broadcasted_iota(jnp.int32, (SLAB_ROWS, 2 * N_PAIRS), 0)
    return ((sub // GATHER_NT) == (lane & 1)).astype(_F32)


def _pipelined_tokens(idx_ref, tbl_ref, tiles, accumulate, drain):
    tb = idx_ref.shape[0]
    n_stage = tb // GATHER_NT
    sets = (tiles[:GATHER_NT], tiles[GATHER_NT:])
    slab_tokens = lambda j: [GATHER_NT * j + n for n in range(GATHER_NT)]

    def stage(j, cur, nxt, acc):
        drain(jnp.maximum(j - 2, 0), acc)
        steps = []
        for n in range(GATHER_NT):
            steps += accumulate(j, n, cur[n], acc, n % 2)
        ahead = slab_tokens(jnp.minimum(j + 1, n_stage - 1))
        part = N_PAIRS // len(steps)
        for i, step in enumerate(steps):
            step()
            _gather_rows(idx_ref, ahead, tbl_ref, nxt, part * i, part * (i + 1))

    def body(j, carry):
        for p in range(2):
            @pl.when((j & 1) == p)
            def _():
                stage(j, sets[p], sets[1 - p], MXU_ACC[p])

        return carry

    @pl.when(pl.program_id(0) == 0)
    def _():
        for acc in MXU_ACC:
            _mxu_pop(acc)

    _gather_rows(idx_ref, slab_tokens(0), tbl_ref, sets[0])
    lax.fori_loop(0, n_stage, body, 0)
    for j in (n_stage - 2, n_stage - 1):
        drain(j, MXU_ACC[j % 2])


def _slab_rows(slab):
    return pl.ds(pl.multiple_of(GATHER_NT * slab, GATHER_NT), GATHER_NT)


def _u_pass_kernel(idx_ref, x_ref, gate_ref, tbl_ref, out_ref, *scratch):
    tiles, ybuf = scratch[:-1], scratch[-1]
    row_masks = _row_masks()
    parity = _parity_mask()

    def accumulate(slab, n, tile, acc, reg):
        x8 = x_ref[_slab_rows(slab), :]
        xs = jnp.concatenate([x8[:, :HALF_D], x8[:, HALF_D:]], axis=0)
        lhs = (xs * row_masks[n][:, :1]).astype(_BF16)
        return _mxu_accumulate(acc, [lhs[:, :256], lhs[:, 256:]], tile, reg, transpose=True)

    def drain(slab, acc):
        z0, z1 = _mxu_pop(acc)
        z = (z0 + z1) * parity
        ybuf[_slab_rows(slab), :] = z[:GATHER_NT] + z[GATHER_NT:]

    _pipelined_tokens(idx_ref, tbl_ref, tiles, accumulate, drain)
    y = ybuf[...]
    lane = lax.broadcasted_iota(jnp.int32, y.shape, 1)
    other = jnp.where((lane & 1) == 0, pltpu.roll(y, 2 * N_PAIRS - 1, axis=1), pltpu.roll(y, 1, axis=1))
    out_ref[...] = gate_ref[...] * _gelu(y + other)


def _v_pass_kernel(idx_ref, w_ref, x1_ref, mod_ref, g_ref, b_ref, tbl_ref, out_ref, *tiles):
    row_masks = _row_masks()
    parity = _parity_mask()

    def accumulate(slab, n, tile, acc, reg):
        w8 = w_ref[_slab_rows(slab), :]
        ws = jnp.concatenate([w8, w8], axis=0)
        lhs = (ws * (row_masks[n] * parity)).astype(_BF16)
        return _mxu_accumulate(acc, [lhs, lhs], tile, reg, transpose=False)

    def drain(slab, acc):
        for h, r in enumerate(_mxu_pop(acc)):
            for b in range(2):
                out_ref[_slab_rows(slab), pl.ds(HALF_D * b + 256 * h, 256)] = r[GATHER_NT * b:GATHER_NT * (b + 1), :]

    _pipelined_tokens(idx_ref, tbl_ref, tiles, accumulate, drain)
    g2 = mod_ref[0][5:6]
    out_ref[...] = _ln(DEEPNORM_ALPHA * x1_ref[...] + g2 * out_ref[...]) * g_ref[...] + b_ref[...]


def _gather_call(kernel_fn, name, n_tok, in_specs, out_spec, out_shape, n_extra_scratch, args):
    tb = GATHER_TB
    tile = pltpu.VMEM((ROW_WORDS * N_PAIRS, 128), jnp.int32)
    return pl.pallas_call(
        kernel_fn,
        out_shape=out_shape,
        grid=(n_tok // tb,),
        in_specs=in_specs,
        out_specs=out_spec,
        scratch_shapes=[tile] * (2 * GATHER_NT) + n_extra_scratch,
        compiler_params=pltpu.CompilerParams(
            dimension_semantics=("arbitrary",),
            vmem_limit_bytes=GATHER_VMEM_BYTES,
        ),
        name=name,
    )(*args)


def _u_pass(idx, h2, gate2, tbl):
    n_tok = idx.shape[0]
    tb = GATHER_TB
    return _gather_call(
        _u_pass_kernel, "peer_u_pass", n_tok,
        [pl.BlockSpec((tb, N_PAIRS), lambda i: (i, 0), memory_space=pltpu.SMEM),
         pl.BlockSpec((tb, D_MODEL), lambda i: (i, 0)),
         pl.BlockSpec((tb, 2 * N_PAIRS), lambda i: (i, 0)),
         _resident()],
        pl.BlockSpec((tb, 2 * N_PAIRS), lambda i: (i, 0)),
        jax.ShapeDtypeStruct((n_tok, 2 * N_PAIRS), _F32),
        [pltpu.VMEM((tb, 2 * N_PAIRS), _F32)],
        (idx, h2, gate2, tbl))


def _v_pass(idx, w2, tbl, x1, mod3, g, b):
    n_tok = idx.shape[0]
    tb = GATHER_TB
    steps_per_seq = n_tok // mod3.shape[0] // tb
    assert steps_per_seq * tb * mod3.shape[0] == n_tok
    vec = pl.BlockSpec((1, D_MODEL), lambda i: (0, 0))
    return _gather_call(
        _v_pass_kernel, "peer_v_pass", n_tok,
        [pl.BlockSpec((tb, N_PAIRS), lambda i: (i, 0), memory_space=pltpu.SMEM),
         pl.BlockSpec((tb, 2 * N_PAIRS), lambda i: (i, 0)),
         pl.BlockSpec((tb, D_MODEL), lambda i: (i, 0)),
         pl.BlockSpec((1, 6, D_MODEL), lambda i: (i // steps_per_seq, 0, 0)),
         vec, vec,
         _resident()],
        pl.BlockSpec((tb, D_MODEL), lambda i: (i, 0)),
        jax.ShapeDtypeStruct((n_tok, D_MODEL), _F32),
        [],
        (idx, w2, x1, mod3, g, b, tbl))


def _final_kernel(x1_ref, ffn_ref, mod_ref, g_ref, b_ref, o_ref):
    g2 = mod_ref[0][5:6]
    o_ref[0] = _ln(DEEPNORM_ALPHA * x1_ref[0] + g2 * ffn_ref[0]) * g_ref[...] + b_ref[...]


def _final_norm(x1, ffn, mod3, g, b):
    bsz, seq, d = x1.shape
    ts = NORM_TS
    blk = pl.BlockSpec((1, ts, d), lambda i, s: (i, s, 0))
    vec = pl.BlockSpec((1, d), lambda i, s: (0, 0))
    return pl.pallas_call(
        _final_kernel,
        out_shape=jax.ShapeDtypeStruct((bsz, seq, d), _F32),
        grid=(bsz, seq // ts),
        in_specs=[blk, blk, pl.BlockSpec((1, 6, d), lambda i, s: (i, 0, 0)), vec, vec],
        out_specs=blk,
        compiler_params=pltpu.CompilerParams(dimension_semantics=("arbitrary", "arbitrary")),
        name="final_norm",
    )(x1, ffn, mod3, g, b)


def kernel(x, c, w_cond, b_cond, w_in, gmlp_ln_g, gmlp_ln_b, w_spatial, b_spatial, conv_w, p_a, p_b, w_o, ln1_g, ln1_b, w_q_peer, sub_keys, expert_u, expert_v, ln2_g, ln2_b):
    bsz, seq, d = x.shape
    n_tok = bsz * seq
    depth = w_cond.shape[0]
    row = lambda v: v.reshape(1, d)
    for l in range(depth):
        mod3 = _cond_proj(c, w_cond[l], b_cond[l]).reshape(bsz, 6, d)
        x1, h2 = _mixer(
            x, mod3, w_in[l].astype(_BF16), row(gmlp_ln_g[l]), row(gmlp_ln_b[l]),
            w_spatial[l].astype(_BF16), b_spatial[l][:, :, None], conv_w[l],
            p_a[l].astype(_BF16), p_b[l].astype(_BF16), w_o[l].astype(_BF16),
            row(ln1_g[l]), row(ln1_b[l]))
        h2 = h2.reshape(n_tok, d)
        keys = sub_keys[l].astype(_BF16).reshape(2 * PEER_HEADS, PEER_N_KEYS, PEER_HALF)
        idx, gate2 = _route(h2, w_q_peer[l].astype(_BF16), keys)
        w2 = _u_pass(idx, h2, gate2, _pack_table(expert_u[l]))
        x = _v_pass(idx, w2, _pack_table(expert_v[l]), x1.reshape(n_tok, d), mod3,
                    row(ln2_g[l]), row(ln2_b[l])).reshape(bsz, seq, d)
    return x
```
